```python
import math
import jax
import jax.numpy as jnp
from jax import lax
import numpy as np

D_MODEL = 1024
BATCH = 16
SEQ = 2048
DEPTH = 2
DEC_BATCH = 32
DEC_SEQ = 1
PAST_LEN = 16384
PAGE_SIZE = 128

HEAD_DIM = 64
MIX_WIDTH = D_MODEL
MOBA_HEADS = (MIX_WIDTH // 2) // HEAD_DIM
DIFF_HEADS = (MIX_WIDTH // 2) // (2 * HEAD_DIM)
MOBA_W = MOBA_HEADS * HEAD_DIM
DIFF_QK_W = DIFF_HEADS * 2 * HEAD_DIM
DIFF_V_W = DIFF_HEADS * 2 * HEAD_DIM
IN_WIDTH = 3 * MOBA_W + 2 * DIFF_QK_W + DIFF_V_W
SPLITS = (MOBA_W, 2 * MOBA_W, 3 * MOBA_W, 3 * MOBA_W + DIFF_QK_W, 3 * MOBA_W + 2 * DIFF_QK_W)
MOBA_BLOCK = 256
MOBA_TOPK = 3
MOBA_Q_CHUNK = 16
ATTN_Q_BLOCK = 128
D_FF = 4 * D_MODEL
ROPE_THETA = 10000.0
EPS = 1e-6
NEG = -1e30

kernel_name = "hymba_moba_diffattn_adaln_step"


def rmsnorm(x, g):
    xf = x.astype(jnp.float32)
    y = xf * lax.rsqrt(jnp.mean(xf * xf, axis=-1, keepdims=True) + EPS)
    return (y * g.astype(jnp.float32)).astype(x.dtype)


def rope(x, pos):
    half = x.shape[-1] // 2
    inv = ROPE_THETA ** (-jnp.arange(half, dtype=jnp.float32) / half)
    ang = pos.astype(jnp.float32)[:, None] * inv[None, :]
    cos = jnp.cos(ang)[None, :, None, :]
    sin = jnp.sin(ang)[None, :, None, :]
    xf = x.astype(jnp.float32)
    x1, x2 = xf[..., :half], xf[..., half:]
    return jnp.concatenate([x1 * cos - x2 * sin, x1 * sin + x2 * cos], axis=-1).astype(x.dtype)


def moba_attention(q, k, v, q_pos):
    B, T, H, D = q.shape
    L = k.shape[1]
    nb = -(-L // MOBA_BLOCK)
    pad = nb * MOBA_BLOCK - L
    kb = jnp.pad(k, ((0, 0), (0, pad), (0, 0), (0, 0))).reshape(B, nb, MOBA_BLOCK, H, D).transpose(0, 3, 1, 2, 4)
    vb = jnp.pad(v, ((0, 0), (0, pad), (0, 0), (0, 0))).reshape(B, nb, MOBA_BLOCK, H, D).transpose(0, 3, 1, 2, 4)
    qh = q.transpose(0, 2, 1, 3)
    q_blk = q_pos // MOBA_BLOCK
    own = jnp.broadcast_to(q_blk[None, None, :, None], (B, H, T, 1))
    n_sel = min(MOBA_TOPK, nb - 1)
    if n_sel > 0:
        k_mean = jnp.mean(kb.astype(jnp.float32), axis=3)
        gate = jnp.einsum('bhtd,bhnd->bhtn', qh.astype(jnp.float32), k_mean)
        fully_past = jnp.arange(nb)[None, :] < q_blk[:, None]
        gate = jnp.where(fully_past[None, None], gate, NEG)
        _, top = lax.top_k(gate, n_sel)
        sel = jnp.concatenate([top, own], axis=-1)
        slot_ok = jnp.concatenate([top < q_blk[None, None, :, None], jnp.ones_like(own, dtype=bool)], axis=-1)
    else:
        sel = own
        slot_ok = jnp.ones_like(own, dtype=bool)
    c = math.gcd(T, MOBA_Q_CHUNK)
    nc = T // c
    scale = 1.0 / math.sqrt(D)
    bi = jnp.arange(B)[:, None, None, None]
    hi = jnp.arange(H)[None, :, None, None]
    offs = jnp.arange(MOBA_BLOCK)

    def chunk(args):
        qc, sc, okc, pc = args
        kg = kb[bi, hi, sc]
        vg = vb[bi, hi, sc]
        kpos = sc[..., None] * MOBA_BLOCK + offs
        mask = okc[..., None] & (kpos <= pc[None, None, :, None, None])
        s = jnp.einsum('bhqd,bhqsjd->bhqsj', qc, kg, preferred_element_type=jnp.float32) * scale
        s = jnp.where(mask, s, NEG)
        p = jax.nn.softmax(s.reshape(B, H, c, -1), axis=-1).reshape(s.shape)
        return jnp.einsum('bhqsj,bhqsjd->bhqd', p.astype(vg.dtype), vg)

    def split(a):
        return jnp.moveaxis(a.reshape(a.shape[0], a.shape[1], nc, c, *a.shape[3:]), 2, 0)

    o = lax.map(chunk, (split(qh), split(sel), split(slot_ok), q_pos.reshape(nc, c)))
    o = jnp.moveaxis(o, 0, 2).reshape(B, H, T, D).transpose(0, 2, 1, 3)
    return o.reshape(B, T, H * D)


def diff_attention(dq, dk, dv, q_pos, lam, lam_init, g_subln):
    B, T = dq.shape[:2]
    L = dk.shape[1]
    q = dq.reshape(B, T, DIFF_HEADS, 2, HEAD_DIM)
    k = dk.reshape(B, L, DIFF_HEADS, 2, HEAD_DIM)
    k_pos = jnp.arange(L)
    c = math.gcd(T, ATTN_Q_BLOCK)
    nblk = T // c
    scale = 1.0 / math.sqrt(HEAD_DIM)
    qb = jnp.moveaxis(q.reshape(B, nblk, c, DIFF_HEADS, 2, HEAD_DIM), 1, 0)
    pb = q_pos.reshape(nblk, c)

    def block(args):
        qc, pc = args
        s = jnp.einsum('bqhid,bkhid->bihqk', qc, k, preferred_element_type=jnp.float32) * scale
        s = jnp.where((k_pos[None, :] <= pc[:, None])[None, None, None], s, NEG)
        p = jax.nn.softmax(s, axis=-1)
        a = p[:, 0] - lam * p[:, 1]
        return jnp.einsum('bhqk,bkhe->bqhe', a.astype(dv.dtype), dv)

    o = lax.map(block, (qb, pb))
    o = jnp.moveaxis(o, 0, 1).reshape(B, T, DIFF_HEADS, 2 * HEAD_DIM)
    o = rmsnorm(o, g_subln) * (1.0 - lam_init)
    return o.reshape(B, T, DIFF_HEADS * 2 * HEAD_DIM)


def trunk_layer(x, c, pos, past, lam, lam_init, w_in, w_out, g_attn, g_mlp, w_ada, b_ada, w_up, w_down, g_subln):
    B, T, _ = x.shape
    mod = jnp.einsum('bd,de->be', jax.nn.silu(c), w_ada) + b_ada
    sh_a, sc_a, gt_a, sh_m, sc_m, gt_m = jnp.split(mod[:, None, :], 6, axis=-1)
    h = rmsnorm(x, g_attn) * (1 + sc_a) + sh_a
    proj = jnp.einsum('btd,de->bte', h, w_in)
    mq, mk, mv, dq, dk, dv = jnp.split(proj, SPLITS, axis=-1)
    mq = rope(mq.reshape(B, T, MOBA_HEADS, HEAD_DIM), pos)
    mk = rope(mk.reshape(B, T, MOBA_HEADS, HEAD_DIM), pos)
    mv = mv.reshape(B, T, MOBA_HEADS, HEAD_DIM)
    dq = rope(dq.reshape(B, T, 2 * DIFF_HEADS, HEAD_DIM), pos)
    dk = rope(dk.reshape(B, T, 2 * DIFF_HEADS, HEAD_DIM), pos)
    dv = dv.reshape(B, T, DIFF_HEADS, 2 * HEAD_DIM)
    if past is None:
        fk_m, fv_m, fk_d, fv_d = mk, mv, dk, dv
    else:
        pk_m, pv_m, pk_d, pv_d = past
        fk_m = jnp.concatenate([pk_m, mk], axis=1)
        fv_m = jnp.concatenate([pv_m, mv], axis=1)
        fk_d = jnp.concatenate([pk_d, dk], axis=1)
        fv_d = jnp.concatenate([pv_d, dv], axis=1)
    o_m = moba_attention(mq, fk_m, fv_m, pos)
    o_d = diff_attention(dq, fk_d, fv_d, pos, lam, lam_init, g_subln)
    o = jnp.einsum('bte,ed->btd', jnp.concatenate([o_m, o_d], axis=-1), w_out)
    x = x + gt_a * o
    h = rmsnorm(x, g_mlp) * (1 + sc_m) + sh_m
    f = jnp.einsum('btf,fd->btd', jnp.square(jax.nn.relu(jnp.einsum('btd,df->btf', h, w_up))), w_down)
    x = x + gt_m * f
    return x, (mk, mv, dk, dv)


def gather_past(pool, l, page_table):
    g = pool[l, page_table]
    return g.reshape(g.shape[0], g.shape[1] * g.shape[2], *g.shape[3:])


def setup_inputs(seed: int = 0) -> dict:
    key = jax.random.key(seed)
    ks = jax.random.split(key, 26)
    f32 = jnp.float32
    n_pages = PAST_LEN // PAGE_SIZE
    n_used = DEC_BATCH * n_pages
    n_pool = n_used + max(1, n_used // 4)

    def nrm(k, shape, s):
        return jax.random.normal(k, shape, f32) * s

    perm = jax.random.permutation(ks[0], n_pool)
    page_table = perm[:n_used].reshape(DEC_BATCH, n_pages).astype(jnp.int32)
    return {
        "x_prompt": nrm(ks[1], (BATCH, SEQ, D_MODEL), 1.0),
        "x_sample": nrm(ks[2], (DEC_BATCH, DEC_SEQ, D_MODEL), 1.0),
        "cache_k_moba": nrm(ks[3], (DEPTH, n_pool, PAGE_SIZE, MOBA_HEADS, HEAD_DIM), 1.0),
        "cache_v_moba": nrm(ks[4], (DEPTH, n_pool, PAGE_SIZE, MOBA_HEADS, HEAD_DIM), 1.0),
        "cache_k_diff": nrm(ks[5], (DEPTH, n_pool, PAGE_SIZE, 2 * DIFF_HEADS, HEAD_DIM), 1.0),
        "cache_v_diff": nrm(ks[6], (DEPTH, n_pool, PAGE_SIZE, DIFF_HEADS, 2 * HEAD_DIM), 1.0),
        "page_table": page_table,
        "c_prompt": nrm(ks[7], (BATCH, D_MODEL), 1.0),
        "c_sample": nrm(ks[8], (DEC_BATCH, D_MODEL), 1.0),
        "w_in": nrm(ks[9], (DEPTH, D_MODEL, IN_WIDTH), D_MODEL ** -0.5),
        "w_out": nrm(ks[10], (DEPTH, MIX_WIDTH, D_MODEL), MIX_WIDTH ** -0.5),
        "g_attn": 1.0 + nrm(ks[11], (DEPTH, D_MODEL), 0.05),
        "g_mlp": 1.0 + nrm(ks[12], (DEPTH, D_MODEL), 0.05),
        "w_ada": nrm(ks[13], (DEPTH, D_MODEL, 6 * D_MODEL), 0.5 * D_MODEL ** -0.5),
        "b_ada": nrm(ks[14], (DEPTH, 6 * D_MODEL), 0.02),
        "w_up": nrm(ks[15], (DEPTH, D_MODEL, D_FF), D_MODEL ** -0.5),
        "w_down": nrm(ks[16], (DEPTH, D_FF, D_MODEL), D_FF ** -0.5),
        "lambda_q1": nrm(ks[17], (DEPTH, HEAD_DIM), 0.1),
        "lambda_k1": nrm(ks[18], (DEPTH, HEAD_DIM), 0.1),
        "lambda_q2": nrm(ks[19], (DEPTH, HEAD_DIM), 0.1),
        "lambda_k2": nrm(ks[20], (DEPTH, HEAD_DIM), 0.1),
        "g_subln": 1.0 + nrm(ks[21], (DEPTH, 2 * HEAD_DIM), 0.05),
        "g_final": 1.0 + nrm(ks[22], (D_MODEL,), 0.05),
    }


def reference(x_prompt, x_sample, cache_k_moba, cache_v_moba, cache_k_diff, cache_v_diff, page_table,
              c_prompt, c_sample, w_in, w_out, g_attn, g_mlp, w_ada, b_ada, w_up, w_down,
              lambda_q1, lambda_k1, lambda_q2, lambda_k2, g_subln, g_final):
    seq = x_prompt.shape[1]
    dec_seq = x_sample.shape[1]
    past_len = page_table.shape[1] * cache_k_moba.shape[2]
    pos_p = jnp.arange(seq, dtype=jnp.int32)
    pos_s = past_len + jnp.arange(dec_seq, dtype=jnp.int32)
    xp, xs = x_prompt, x_sample
    kmp, vmp, kdp, vdp = [], [], [], []
    kms, vms, kds, vds = [], [], [], []
    for l in range(DEPTH):
        lam_init = 0.8 - 0.6 * math.exp(-0.3 * l)
        lam = (jnp.exp(jnp.sum(lambda_q1[l].astype(jnp.float32) * lambda_k1[l].astype(jnp.float32)))
               - jnp.exp(jnp.sum(lambda_q2[l].astype(jnp.float32) * lambda_k2[l].astype(jnp.float32)))
               + lam_init)
        past = (gather_past(cache_k_moba, l, page_table), gather_past(cache_v_moba, l, page_table),
                gather_past(cache_k_diff, l, page_table), gather_past(cache_v_diff, l, page_table))
        xp, rp = trunk_layer(xp, c_prompt, pos_p, None, lam, lam_init, w_in[l], w_out[l], g_attn[l], g_mlp[l],
                             w_ada[l], b_ada[l], w_up[l], w_down[l], g_subln[l])
        xs, rs = trunk_layer(xs, c_sample, pos_s, past, lam, lam_init, w_in[l], w_out[l], g_attn[l], g_mlp[l],
                             w_ada[l], b_ada[l], w_up[l], w_down[l], g_subln[l])
        kmp.append(rp[0]); vmp.append(rp[1]); kdp.append(rp[2]); vdp.append(rp[3])
        kms.append(rs[0]); vms.append(rs[1]); kds.append(rs[2]); vds.append(rs[3])
    y_prompt = rmsnorm(xp, g_final)
    y_sample = rmsnorm(xs, g_final)
    return (y_prompt, y_sample,
            jnp.stack(kmp), jnp.stack(vmp), jnp.stack(kdp), jnp.stack(vdp),
            jnp.stack(kms), jnp.stack(vms), jnp.stack(kds), jnp.stack(vds))
```

```python
import functools
import math

import jax
import jax.numpy as jnp
from jax import lax
from jax.experimental import pallas as pl
from jax.experimental.pallas import tpu as pltpu

F32 = jnp.float32
BF16 = jnp.bfloat16
HIGHEST = lax.Precision.HIGHEST

D_MODEL = 1024
HEAD_DIM = 64
HALF_DIM = HEAD_DIM // 2
MOBA_HEADS = 8
DIFF_HEADS = 4
GROUP_W = 512
MOBA_BLOCK = 256
MOBA_TOPK = 3
D_FF = 4 * D_MODEL
ROPE_THETA = 10000.0
EPS = 1e-6
NEG = -1e30
Q_SCALE = 1.0 / math.sqrt(HEAD_DIM)

LANES = 128
SUBLANES = 8
VMEM_LIMIT_BYTES = 52 * 1024 * 1024

PROJ_TM = 512
MLP_TM = 1024
MLP_TF = 1024
ADA_TN = 1536


def _cparams(semantics):
    return pltpu.CompilerParams(dimension_semantics=semantics, vmem_limit_bytes=VMEM_LIMIT_BYTES)


def _rmsnorm(x, g):
    return x * lax.rsqrt(jnp.mean(x * x, axis=-1, keepdims=True) + EPS) * g


def _dot(a, b, exact):
    if exact:
        return jnp.dot(a.astype(F32), b.astype(F32), precision=HIGHEST, preferred_element_type=F32)
    return jnp.dot(a.astype(BF16), b.astype(BF16), preferred_element_type=F32)


def _dot_nt(a, b):
    return lax.dot_general(a, b, (((1,), (1,)), ((), ())), preferred_element_type=F32)


def _swap_halves(x):
    lane = lax.broadcasted_iota(jnp.int32, x.shape, 1)
    first_half = (lane % HEAD_DIM) < HALF_DIM
    return jnp.where(first_half, pltpu.roll(x, LANES - HALF_DIM, 1), pltpu.roll(x, HALF_DIM, 1))


def _rope_rows(p, cos, sin_signed):
    return p * cos + _swap_halves(p) * sin_signed


def _ada_kernel(c_ref, w_ref, b_ref, o_ref):
    c = c_ref[...]
    silu = c / (1.0 + jnp.exp(-c))
    o_ref[0] = _dot(silu, w_ref[0], True) + b_ref[0]


def _ada_call(c_all, w_ada, b_ada):
    depth, d, n = w_ada.shape
    rows = c_all.shape[0]
    return pl.pallas_call(
        _ada_kernel,
        grid=(depth, n // ADA_TN),
        in_specs=[
            pl.BlockSpec((rows, d), lambda l, j: (0, 0)),
            pl.BlockSpec((1, d, ADA_TN), lambda l, j: (l, 0, j)),
            pl.BlockSpec((1, 1, ADA_TN), lambda l, j: (l, 0, j)),
        ],
        out_specs=pl.BlockSpec((1, rows, ADA_TN), lambda l, j: (l, 0, j)),
        out_shape=jax.ShapeDtypeStruct((depth, rows, n), F32),
        compiler_params=_cparams(("parallel", "parallel")),
        name="adaln",
    )(c_all, w_ada, b_ada.reshape(depth, 1, n))


def _proj_kernel(x_ref, g_ref, sc_ref, sh_ref, wa_ref, wbt_ref, cr_ref, sr_ref, ct_ref, st_ref, *rest):
    qm_ref, qd_ref, vd_ref, ktm_ref, vtm_ref, ktd_ref = rest[-6:]
    tm = x_ref.shape[1]
    h = (_rmsnorm(x_ref[0], g_ref[...]) * (1.0 + sc_ref[0]) + sh_ref[0]).astype(BF16)

    pa = jnp.dot(h, wa_ref[...], preferred_element_type=F32)
    cos_r = cr_ref[...]
    sin_r = sr_ref[...]
    for c in range(GROUP_W // LANES):
        lo = c * LANES
        qm_ref[0, :, lo:lo + LANES] = (_rope_rows(pa[:, lo:lo + LANES], cos_r, sin_r) * Q_SCALE).astype(BF16)
        qd_ref[0, :, lo:lo + LANES] = (
            _rope_rows(pa[:, GROUP_W + lo:GROUP_W + lo + LANES], cos_r, sin_r) * Q_SCALE).astype(BF16)
    for hd in range(DIFF_HEADS):
        lo = 2 * GROUP_W + hd * LANES
        vd_ref[0, 0, pl.ds(hd, tm, stride=DIFF_HEADS), :] = pa[:, lo:lo + LANES]

    pb = _dot_nt(wbt_ref[...], h)
    cos_t = ct_ref[...]
    sin_t = st_ref[...]
    vtm_ref[0, 0] = pb[GROUP_W:2 * GROUP_W]
    for src, dst in ((0, ktm_ref), (2 * GROUP_W, ktd_ref)):
        for hd in range(GROUP_W // HEAD_DIM):
            r0 = src + hd * HEAD_DIM
            x1 = pb[r0:r0 + HALF_DIM]
            x2 = pb[r0 + HALF_DIM:r0 + HEAD_DIM]
            o0 = hd * HEAD_DIM
            dst[0, 0, o0:o0 + HALF_DIM, :] = x1 * cos_t - x2 * sin_t
            dst[0, 0, o0 + HALF_DIM:o0 + HEAD_DIM, :] = x1 * sin_t + x2 * cos_t


def _proj_call(layer, depth, x, g, sc, sh, wa, wbt, cos_r, sin_r, cos_t, sin_t, layered):
    b, t, d = x.shape
    tm = min(PROJ_TM, t)
    in_specs = [
        pl.BlockSpec((1, tm, d), lambda i, j: (i, j, 0)),
        pl.BlockSpec((1, d), lambda i, j: (0, 0)),
        pl.BlockSpec((1, 1, d), lambda i, j: (i, 0, 0)),
        pl.BlockSpec((1, 1, d), lambda i, j: (i, 0, 0)),
        pl.BlockSpec(wa.shape, lambda i, j: (0, 0)),
        pl.BlockSpec(wbt.shape, lambda i, j: (0, 0)),
        pl.BlockSpec((tm, LANES), lambda i, j: (j, 0)),
        pl.BlockSpec((tm, LANES), lambda i, j: (j, 0)),
        pl.BlockSpec((HALF_DIM, tm), lambda i, j: (0, j)),
        pl.BlockSpec((HALF_DIM, tm), lambda i, j: (0, j)),
    ]
    args = [x, g, sc, sh, wa, wbt, cos_r, sin_r, cos_t, sin_t]
    aliases = {}
    for k, arr in enumerate(layered):
        in_specs.append(pl.BlockSpec(memory_space=pl.ANY))
        aliases[len(args)] = 2 + k
        args.append(arr)
    kt_shape = jax.ShapeDtypeStruct((depth, b, GROUP_W, t), F32)
    kt_spec = pl.BlockSpec((1, 1, GROUP_W, tm), lambda i, j: (layer, i, 0, j))
    return pl.pallas_call(
        _proj_kernel,
        grid=(b, t // tm),
        in_specs=in_specs,
        out_specs=[
            pl.BlockSpec((1, tm, GROUP_W), lambda i, j: (i, j, 0)),
            pl.BlockSpec((1, tm, GROUP_W), lambda i, j: (i, j, 0)),
            pl.BlockSpec((1, 1, DIFF_HEADS * tm, LANES), lambda i, j: (layer, i, j, 0)),
            kt_spec, kt_spec, kt_spec,
        ],
        out_shape=[
            jax.ShapeDtypeStruct((b, t, GROUP_W), BF16),
            jax.ShapeDtypeStruct((b, t, GROUP_W), BF16),
            jax.ShapeDtypeStruct((depth, b, DIFF_HEADS * t, LANES), F32),
            kt_shape, kt_shape, kt_shape,
        ],
        input_output_aliases=aliases,
        compiler_params=_cparams(("parallel", "parallel")),
        name=f"proj_l{layer}",
    )(*args)


def _online_update(s, v_nt, v_nn, m_ref, l_ref, acc_ref, idx):
    m_prev = m_ref[idx]
    m_next = jnp.maximum(m_prev, jnp.max(s, axis=1, keepdims=True))
    alpha = jnp.exp(m_prev - m_next)
    p = jnp.exp(s - jnp.concatenate([m_next] * (s.shape[1] // LANES), axis=1))
    l_ref[idx] = alpha * l_ref[idx] + jnp.sum(p, axis=1, keepdims=True)
    pb = p.astype(BF16)
    pv = _dot_nt(pb, v_nt) if v_nn is None else jnp.dot(pb, v_nn, preferred_element_type=F32)
    acc_ref[idx] = acc_ref[idx] * alpha + pv
    m_ref[idx] = m_next


def _first_update(s, v_nt, v_nn, m_ref, l_ref, acc_ref, idx):
    m = jnp.max(s, axis=1, keepdims=True)
    p = jnp.exp(s - m)
    pb = p.astype(BF16)
    acc_ref[idx] = _dot_nt(pb, v_nt) if v_nn is None else jnp.dot(pb, v_nn, preferred_element_type=F32)
    m_ref[idx] = jnp.broadcast_to(m, m_ref.shape[1:])
    l_ref[idx] = jnp.broadcast_to(jnp.sum(p, axis=1, keepdims=True), l_ref.shape[1:])


def _top_blocks(gate, n_valid):
    lane = lax.broadcasted_iota(jnp.int32, gate.shape, 1)
    lane_f = lane.astype(F32)
    g = jnp.where(lane < n_valid, gate, -jnp.inf)
    sel = jnp.zeros(gate.shape, F32)
    for _ in range(MOBA_TOPK):
        m = jnp.max(g, axis=1, keepdims=True)
        first = jnp.min(jnp.where(g == m, lane_f, float(LANES)), axis=1, keepdims=True)
        pick = (lane_f == first) & (m > -jnp.inf)
        sel = jnp.where(pick, 1.0, sel)
        g = jnp.where(pick, -jnp.inf, g)
    return sel


def _moba_kernel(q_ref, kt_ref, vt_ref, o_ref, kt_s, vt_s, kmh_s, kml_s, m_s, l_s, acc_s):
    qi = pl.program_id(2)
    n_blk, _, blk = kt_s.shape
    tq = q_ref.shape[1]

    @pl.when(qi == 0)
    def _stage():
        lane = lax.broadcasted_iota(jnp.int32, (LANES, LANES), 1)
        km = jnp.zeros((LANES, LANES), F32)
        for n in range(n_blk):
            kb = kt_ref[0, 0, :, n * blk:(n + 1) * blk]
            kt_s[n] = kb.astype(BF16)
            vt_s[n] = vt_ref[0, 0, :, n * blk:(n + 1) * blk].astype(BF16)
            km = jnp.where(lane == n, jnp.sum(kb, axis=1, keepdims=True) * (1.0 / blk), km)
        hi = km.astype(BF16)
        kmh_s[...] = hi
        kml_s[...] = (km - hi.astype(F32)).astype(BF16)

    q = q_ref[0]
    lane = lax.broadcasted_iota(jnp.int32, (tq, LANES), 1)
    low = lane < HEAD_DIM
    zero = jnp.zeros_like(q)
    q_heads = (jnp.where(low, q, zero), jnp.where(low, zero, q))
    sels = [
        _top_blocks(jnp.dot(qh, kmh_s[...], preferred_element_type=F32)
                    + jnp.dot(qh, kml_s[...], preferred_element_type=F32), qi)
        for qh in q_heads
    ]

    row = lax.broadcasted_iota(jnp.int32, (tq, blk), 0)
    col = lax.broadcasted_iota(jnp.int32, (tq, blk), 1)
    causal = col <= row
    kt_d = kt_s[qi]
    vt_d = vt_s[qi]
    for idx, qh in enumerate(q_heads):
        s = jnp.where(causal, jnp.dot(qh, kt_d, preferred_element_type=F32), NEG)
        _first_update(s, vt_d, None, m_s, l_s, acc_s, idx)

    def past_block(n, carry):
        kt = kt_s[n]
        vt = vt_s[n]
        for idx, qh in enumerate(q_heads):
            chosen = jnp.max(jnp.where(lane == n, sels[idx], 0.0), axis=1, keepdims=True)
            s = jnp.where(chosen > 0.0, jnp.dot(qh, kt, preferred_element_type=F32), NEG)
            _online_update(s, vt, None, m_s, l_s, acc_s, idx)
        return carry

    lax.fori_loop(0, qi, past_block, 0)
    o_ref[0] = jnp.where(low, acc_s[0] / l_s[0], acc_s[1] / l_s[1]).astype(o_ref.dtype)


def _moba_call(layer, q, kt_all, vt_all):
    b, t, _ = q.shape
    blk = MOBA_BLOCK
    n_blk = t // blk
    kv_spec = pl.BlockSpec((1, 1, LANES, t), lambda i, h, j: (layer, i, h, 0))
    return pl.pallas_call(
        _moba_kernel,
        grid=(b, GROUP_W // LANES, n_blk),
        in_specs=[pl.BlockSpec((1, blk, LANES), lambda i, h, j: (i, j, h)), kv_spec, kv_spec],
        out_specs=pl.BlockSpec((1, blk, LANES), lambda i, h, j: (i, j, h)),
        out_shape=jax.ShapeDtypeStruct((b, t, GROUP_W), BF16),
        scratch_shapes=[
            pltpu.VMEM((n_blk, LANES, blk), BF16),
            pltpu.VMEM((n_blk, LANES, blk), BF16),
            pltpu.VMEM((LANES, LANES), BF16),
            pltpu.VMEM((LANES, LANES), BF16),
            pltpu.VMEM((2, blk, LANES), F32),
            pltpu.VMEM((2, blk, LANES), F32),
            pltpu.VMEM((2, blk, LANES), F32),
        ],
        compiler_params=_cparams(("parallel", "parallel", "arbitrary")),
        name=f"moba_l{layer}",
    )(q, kt_all, vt_all)


def _lambda(lam_ref, lam_init):
    a = jnp.sum(lam_ref[0:1, :] * lam_ref[1:2, :], axis=1, keepdims=True)
    b = jnp.sum(lam_ref[2:3, :] * lam_ref[3:4, :], axis=1, keepdims=True)
    return jnp.exp(a) - jnp.exp(b) + lam_init


def _diff_kernel(lam_ref, g_ref, q_ref, kt_ref, v_ref, o_ref, kt_s, v_s, m_s, l_s, acc_s, *, lam_init):
    hd = pl.program_id(1)
    qi = pl.program_id(2)
    n_blk, _, blk = kt_s.shape
    tq = q_ref.shape[1]

    @pl.when(qi == 0)
    def _stage():
        for n in range(n_blk):
            kt_s[n] = kt_ref[0, 0, :, n * blk:(n + 1) * blk].astype(BF16)
            v_s[n] = v_ref[0, 0, pl.ds(n * blk * DIFF_HEADS + hd, blk, stride=DIFF_HEADS), :].astype(BF16)

    q = q_ref[0]
    lane = lax.broadcasted_iota(jnp.int32, (tq, LANES), 1)
    low = lane < HEAD_DIM
    zero = jnp.zeros_like(q)
    q_maps = (jnp.where(low, q, zero), jnp.where(low, zero, q))

    row = lax.broadcasted_iota(jnp.int32, (tq, blk), 0)
    col = lax.broadcasted_iota(jnp.int32, (tq, blk), 1)
    causal = col <= row
    kt_d = kt_s[qi]
    v_d = v_s[qi]
    for idx, qm in enumerate(q_maps):
        s = jnp.where(causal, jnp.dot(qm, kt_d, preferred_element_type=F32), NEG)
        _first_update(s, None, v_d, m_s, l_s, acc_s, idx)

    def past_block(n, carry):
        kt = kt_s[n]
        v = v_s[n]
        for idx, qm in enumerate(q_maps):
            _online_update(jnp.dot(qm, kt, preferred_element_type=F32), None, v, m_s, l_s, acc_s, idx)
        return carry

    lax.fori_loop(0, qi, past_block, 0)
    o = acc_s[0] / l_s[0] - _lambda(lam_ref, lam_init) * (acc_s[1] / l_s[1])
    o_ref[0] = (_rmsnorm(o, g_ref[...]) * (1.0 - lam_init)).astype(o_ref.dtype)


def _diff_call(layer, lam_init, lamv, g_sub, q, kt_all, v_all):
    b, t, _ = q.shape
    blk = MOBA_BLOCK
    n_blk = t // blk
    return pl.pallas_call(
        functools.partial(_diff_kernel, lam_init=lam_init),
        grid=(b, DIFF_HEADS, n_blk),
        in_specs=[
            pl.BlockSpec(lamv.shape, lambda i, h, j: (0, 0)),
            pl.BlockSpec(g_sub.shape, lambda i, h, j: (0, 0)),
            pl.BlockSpec((1, blk, LANES), lambda i, h, j: (i, j, h)),
            pl.BlockSpec((1, 1, LANES, t), lambda i, h, j: (layer, i, h, 0)),
            pl.BlockSpec((1, 1, DIFF_HEADS * t, LANES), lambda i, h, j: (layer, i, 0, 0)),
        ],
        out_specs=pl.BlockSpec((1, blk, LANES), lambda i, h, j: (i, j, h)),
        out_shape=jax.ShapeDtypeStruct((b, t, GROUP_W), BF16),
        scratch_shapes=[
            pltpu.VMEM((n_blk, LANES, blk), BF16),
            pltpu.VMEM((n_blk, blk, LANES), BF16),
            pltpu.VMEM((2, blk, LANES), F32),
            pltpu.VMEM((2, blk, LANES), F32),
            pltpu.VMEM((2, blk, LANES), F32),
        ],
        compiler_params=_cparams(("parallel", "parallel", "arbitrary")),
        name=f"diff_l{layer}",
    )(lamv, g_sub, q, kt_all, v_all)


def _outproj_kernel(x_ref, om_ref, od_ref, w_ref, gt_ref, g_ref, sc_ref, sh_ref, x1_ref, h2_ref, *, exact):
    o = _dot(om_ref[0], w_ref[0:GROUP_W, :], exact) + _dot(od_ref[0], w_ref[GROUP_W:2 * GROUP_W, :], exact)
    x1 = x_ref[0] + gt_ref[0] * o
    x1_ref[0] = x1
    h2_ref[0] = (_rmsnorm(x1, g_ref[...]) * (1.0 + sc_ref[0]) + sh_ref[0]).astype(h2_ref.dtype)


def _mod_spec(mod, tm):
    d = mod.shape[-1]
    if mod.shape[1] == 1:
        return pl.BlockSpec((1, 1, d), lambda i, j, *_: (i, 0, 0))
    return pl.BlockSpec((1, tm, d), lambda i, j, *_: (i, j, 0))


def _outproj_call(name, x, om, od, w, gt, g, sc, sh, tm, exact):
    grp, rows, d = x.shape
    row_spec = lambda width: pl.BlockSpec((1, tm, width), lambda i, j: (i, j, 0))
    return pl.pallas_call(
        functools.partial(_outproj_kernel, exact=exact),
        grid=(grp, rows // tm),
        in_specs=[
            row_spec(d), row_spec(GROUP_W), row_spec(GROUP_W),
            pl.BlockSpec(w.shape, lambda i, j: (0, 0)),
            _mod_spec(gt, tm),
            pl.BlockSpec((1, d), lambda i, j: (0, 0)),
            _mod_spec(sc, tm), _mod_spec(sh, tm),
        ],
        out_specs=[row_spec(d), row_spec(d)],
        out_shape=[jax.ShapeDtypeStruct(x.shape, F32), jax.ShapeDtypeStruct(x.shape, F32 if exact else BF16)],
        compiler_params=_cparams(("parallel", "parallel")),
        name=name,
    )(x, om, od, w, gt, g, sc, sh)


def _mlp_kernel(h_ref, x1_ref, wu_ref, wd_ref, gt_ref, gf_ref, o_ref, acc_s, *, exact, final_norm):
    f = pl.program_id(2)

    @pl.when(f == 0)
    def _zero():
        acc_s[...] = jnp.zeros_like(acc_s)

    u = jnp.maximum(_dot(h_ref[0], wu_ref[...], exact), 0.0)
    acc_s[...] += _dot(u * u, wd_ref[...], exact)

    @pl.when(f == pl.num_programs(2) - 1)
    def _finish():
        x2 = x1_ref[0] + gt_ref[0] * acc_s[...]
        o_ref[0] = _rmsnorm(x2, gf_ref[...]) if final_norm else x2


def _mlp_call(name, h, x1, wu, wd, gt, gf, tm, exact, final_norm):
    grp, rows, d = x1.shape
    ff = wu.shape[1]
    tf = min(MLP_TF, ff)
    row_spec = pl.BlockSpec((1, tm, d), lambda i, j, f: (i, j, 0))
    return pl.pallas_call(
        functools.partial(_mlp_kernel, exact=exact, final_norm=final_norm),
        grid=(grp, rows // tm, ff // tf),
        in_specs=[
            row_spec, row_spec,
            pl.BlockSpec((d, tf), lambda i, j, f: (0, f)),
            pl.BlockSpec((tf, d), lambda i, j, f: (f, 0)),
            _mod_spec(gt, tm),
            pl.BlockSpec((1, d), lambda i, j, f: (0, 0)),
        ],
        out_specs=row_spec,
        out_shape=jax.ShapeDtypeStruct(x1.shape, F32),
        scratch_shapes=[pltpu.VMEM((tm, d), F32)],
        compiler_params=_cparams(("parallel", "parallel", "arbitrary")),
        name=name,
    )(h, x1, wu, wd, gt, gf)


def _sproj_kernel(x_ref, g_ref, sc_ref, sh_ref, w_ref, cr_ref, sr_ref, o_ref):
    j = pl.program_id(0)
    h = _rmsnorm(x_ref[...], g_ref[...]) * (1.0 + sc_ref[...]) + sh_ref[...]
    p = _dot(h, w_ref[...], True)
    is_q = (j == 0) | (j == 3)
    is_rope = is_q | (j == 1) | (j == 4)
    scale = jnp.where(is_q, Q_SCALE, 1.0)
    cos_r = cr_ref[...]
    sin_r = sr_ref[...]
    for c in range(GROUP_W // LANES):
        pc = p[:, c * LANES:(c + 1) * LANES]
        o_ref[:, c * LANES:(c + 1) * LANES] = jnp.where(is_rope, _rope_rows(pc, cos_r, sin_r), pc) * scale


def _sproj_call(layer, x, g, sc, sh, w, cos_r, sin_r):
    rows, d = x.shape
    n = w.shape[1]
    full = lambda a: pl.BlockSpec(a.shape, lambda j: (0, 0))
    return pl.pallas_call(
        _sproj_kernel,
        grid=(n // GROUP_W,),
        in_specs=[full(x), full(g), full(sc), full(sh), pl.BlockSpec((d, GROUP_W), lambda j: (0, j)),
                  full(cos_r), full(sin_r)],
        out_specs=pl.BlockSpec((rows, GROUP_W), lambda j: (0, j)),
        out_shape=jax.ShapeDtypeStruct((rows, n), F32),
        compiler_params=_cparams(("parallel",)),
        name=f"sproj_l{layer}",
    )(x, g, sc, sh, w, cos_r, sin_r)


def _head_scores(kt, qrep):
    n_heads = kt.shape[0] // HEAD_DIM
    prod = (kt * qrep).reshape(n_heads, HEAD_DIM // SUBLANES, SUBLANES, LANES)
    return jnp.sum(jnp.sum(prod, axis=1), axis=1)


def _chunk_copies(pt_ref, srcs, bufs, sem, layer, seq, chunk, slot, n):
    out = []
    for k, (src, buf) in enumerate(zip(srcs, bufs)):
        for i in range(n):
            page = pt_ref[seq, chunk * n + i]
            out.append(pltpu.make_async_copy(src.at[layer, page], buf.at[slot, i], sem.at[k, slot]))
    return out


def _stream_chunks(pt_ref, srcs, bufs, sem, layer, n):
    seq = pl.program_id(0)
    chunk = pl.program_id(1)
    n_chunk = pl.num_programs(1)
    step = seq * n_chunk + chunk
    slot = lax.rem(step, 2)

    @pl.when(step == 0)
    def _prime():
        for cp in _chunk_copies(pt_ref, srcs, bufs, sem, layer, 0, 0, 0, n):
            cp.start()

    @pl.when(step + 1 < pl.num_programs(0) * n_chunk)
    def _prefetch():
        wrap = chunk + 1 == n_chunk
        nseq = jnp.where(wrap, seq + 1, seq)
        nchunk = jnp.where(wrap, 0, chunk + 1)
        for cp in _chunk_copies(pt_ref, srcs, bufs, sem, layer, nseq, nchunk, 1 - slot, n):
            cp.start()

    for cp in _chunk_copies(pt_ref, srcs, bufs, sem, layer, seq, chunk, slot, n):
        cp.wait()
    return slot


def _dec_gate_kernel(pt_ref, qrep_ref, kc_ref, sel_ref, buf, sem, g_s, *, layer, n):
    chunk = pl.program_id(1)
    slot = _stream_chunks(pt_ref, (kc_ref,), (buf,), sem, layer, n)
    qrep = qrep_ref[0]
    per_blk = MOBA_BLOCK // buf.shape[3]
    for k in range(n // per_blk):
        pages = buf[slot, per_blk * k]
        for extra in range(1, per_blk):
            pages = pages + buf[slot, per_blk * k + extra]
        g_s[chunk * (n // per_blk) + k] = _head_scores(pages, qrep)

    @pl.when(chunk == pl.num_programs(1) - 1)
    def _select():
        n_blk = g_s.shape[0]
        flat = g_s[...].reshape(n_blk * MOBA_HEADS, LANES)
        tot = jnp.broadcast_to(jnp.sum(flat, axis=1, keepdims=True), flat.shape).reshape(g_s.shape)
        blk = lax.broadcasted_iota(jnp.int32, g_s.shape, 0).astype(F32)
        for r in range(MOBA_TOPK):
            m = jnp.max(tot, axis=0, keepdims=True)
            first = jnp.min(jnp.where(tot == m, blk, float(n_blk)), axis=0, keepdims=True)
            sel_ref[0, r] = first[0].astype(jnp.int32)
            tot = jnp.where(blk == first, -jnp.inf, tot)


def _dec_gate_call(layer, page_table, qrep, kc, n):
    nb, n_pages = page_table.shape
    rows, page = kc.shape[2], kc.shape[3]
    n_blk = n_pages * page // MOBA_BLOCK
    return pl.pallas_call(
        functools.partial(_dec_gate_kernel, layer=layer, n=n),
        grid_spec=pltpu.PrefetchScalarGridSpec(
            num_scalar_prefetch=1,
            grid=(nb, n_pages // n),
            in_specs=[pl.BlockSpec((1, rows, LANES), lambda i, c, pt: (i, 0, 0)),
                      pl.BlockSpec(memory_space=pl.ANY)],
            out_specs=pl.BlockSpec((1, MOBA_TOPK, MOBA_HEADS, LANES), lambda i, c, pt: (i, 0, 0, 0)),
            scratch_shapes=[pltpu.VMEM((2, n, rows, page), F32), pltpu.SemaphoreType.DMA((1, 2)),
                            pltpu.VMEM((n_blk, MOBA_HEADS, LANES), F32)],
        ),
        out_shape=jax.ShapeDtypeStruct((nb, MOBA_TOPK, MOBA_HEADS, LANES), jnp.int32),
        compiler_params=_cparams(("arbitrary", "arbitrary")),
        name=f"dec_gate_l{layer}",
    )(page_table, qrep, kc)


def _dec_moba_kernel(pt_ref, sel_ref, qrep_ref, knrep_ref, vnrep_ref, kc_ref, vc_ref, o_ref, kbuf, vbuf, sem,
                     *, layer):
    seq = pl.program_id(0)
    slot = lax.rem(seq, 2)
    per_blk = MOBA_BLOCK // kbuf.shape[3]
    per_head = MOBA_TOPK * per_blk

    def copies(s, sl):
        out = []
        for hd in range(MOBA_HEADS):
            rows = pl.ds(hd * HEAD_DIM, HEAD_DIM)
            for r in range(MOBA_TOPK):
                blk = sel_ref[s, r * MOBA_HEADS + hd]
                for pg in range(per_blk):
                    page = pt_ref[s, per_blk * blk + pg]
                    j = hd * per_head + r * per_blk + pg
                    out.append(pltpu.make_async_copy(kc_ref.at[layer, page, rows], kbuf.at[sl, j], sem.at[0, sl]))
                    out.append(pltpu.make_async_copy(vc_ref.at[layer, page, rows], vbuf.at[sl, j], sem.at[1, sl]))
        return out

    @pl.when(seq == 0)
    def _prime():
        for cp in copies(0, 0):
            cp.start()

    @pl.when(seq + 1 < pl.num_programs(0))
    def _prefetch():
        for cp in copies(seq + 1, 1 - slot):
            cp.start()

    for cp in copies(seq, slot):
        cp.wait()

    for hd in range(MOBA_HEADS):
        r0 = hd * HEAD_DIM
        q = qrep_ref[0, r0:r0 + HEAD_DIM, :]
        s_self = jnp.sum(q * knrep_ref[0, r0:r0 + HEAD_DIM, :], axis=0, keepdims=True)
        s_pages = [jnp.sum(kbuf[slot, hd * per_head + j] * q, axis=0, keepdims=True) for j in range(per_head)]
        m = s_self
        for s in s_pages:
            m = jnp.maximum(m, jnp.max(s, axis=1, keepdims=True))
        p_self = jnp.exp(s_self - m)
        l = p_self
        o = vnrep_ref[0, r0:r0 + HEAD_DIM, :] * p_self
        for j, s in enumerate(s_pages):
            p = jnp.exp(s - m)
            l = l + jnp.sum(p, axis=1, keepdims=True)
            o = o + jnp.sum(vbuf[slot, hd * per_head + j] * p, axis=1, keepdims=True)
        o_ref[0, r0:r0 + HEAD_DIM, :] = o / l


def _dec_moba_call(layer, page_table, sel, qrep, knrep, vnrep, kc, vc):
    nb = page_table.shape[0]
    rows, page = kc.shape[2], kc.shape[3]
    n_slab = MOBA_HEADS * MOBA_TOPK * (MOBA_BLOCK // page)
    rep_spec = pl.BlockSpec((1, rows, LANES), lambda i, pt, sl: (i, 0, 0))
    return pl.pallas_call(
        functools.partial(_dec_moba_kernel, layer=layer),
        grid_spec=pltpu.PrefetchScalarGridSpec(
            num_scalar_prefetch=2,
            grid=(nb,),
            in_specs=[rep_spec, rep_spec, rep_spec,
                      pl.BlockSpec(memory_space=pl.ANY), pl.BlockSpec(memory_space=pl.ANY)],
            out_specs=rep_spec,
            scratch_shapes=[pltpu.VMEM((2, n_slab, HEAD_DIM, page), F32),
                            pltpu.VMEM((2, n_slab, HEAD_DIM, page), F32),
                            pltpu.SemaphoreType.DMA((2, 2))],
        ),
        out_shape=jax.ShapeDtypeStruct((nb, rows, LANES), F32),
        compiler_params=_cparams(("arbitrary",)),
        name=f"dec_moba_l{layer}",
    )(page_table, sel, qrep, knrep, vnrep, kc, vc)


def _dec_diff_kernel(pt_ref, lam_ref, g_ref, qrep_ref, knrep_ref, vnew_ref, kc_ref, vc_ref, o_ref,
                     kbuf, vbuf, sem, m_s, l_s, acc_s, *, layer, n, lam_init):
    chunk = pl.program_id(1)
    slot = _stream_chunks(pt_ref, (kc_ref, vc_ref), (kbuf, vbuf), sem, layer, n)
    page = kbuf.shape[3]
    qrep = qrep_ref[0]

    @pl.when(chunk == 0)
    def _init():
        m_s[...] = jnp.full(m_s.shape, -jnp.inf, F32)
        l_s[...] = jnp.zeros_like(l_s)
        acc_s[...] = jnp.zeros_like(acc_s)

    def one_page(i, carry):
        s = _head_scores(kbuf[slot, i], qrep)
        m_prev = m_s[...]
        m_next = jnp.maximum(m_prev, jnp.max(s, axis=1, keepdims=True))
        alpha = jnp.exp(m_prev - m_next)
        p = jnp.exp(s - m_next)
        l_s[...] = alpha * l_s[...] + jnp.sum(p, axis=1, keepdims=True)
        m_s[...] = m_next
        pb = p.astype(BF16)
        for hd in range(DIFF_HEADS):
            v = vbuf[slot, i, pl.ds(hd, page, stride=DIFF_HEADS), :].astype(BF16)
            acc_s[hd] = acc_s[hd] * alpha + jnp.dot(pb, v, preferred_element_type=F32)
        return carry

    lax.fori_loop(0, n, one_page, 0)

    @pl.when(chunk == pl.num_programs(1) - 1)
    def _finish():
        n_map = 2 * DIFF_HEADS
        s_self = jnp.sum((qrep * knrep_ref[0]).reshape(n_map, HEAD_DIM, LANES), axis=1)
        m_prev = m_s[...]
        m_fin = jnp.maximum(m_prev, s_self)
        alpha = jnp.exp(m_prev - m_fin)
        p_self = jnp.exp(s_self - m_fin)
        l_fin = alpha * l_s[...] + p_self
        lam = _lambda(lam_ref, lam_init)
        for hd in range(DIFF_HEADS):
            maps = (acc_s[hd] * alpha + p_self * vnew_ref[0, hd:hd + 1, :]) / l_fin
            o = maps[2 * hd:2 * hd + 1, :] - lam * maps[2 * hd + 1:2 * hd + 2, :]
            o_ref[0, hd:hd + 1, :] = _rmsnorm(o, g_ref[...]) * (1.0 - lam_init)


def _dec_diff_call(layer, lam_init, page_table, lamv, g_sub, qrep, knrep, vnew, kc, vc, n):
    nb, n_pages = page_table.shape
    rows, page = kc.shape[2], kc.shape[3]
    n_map = 2 * DIFF_HEADS
    full = lambda a: pl.BlockSpec(a.shape, lambda i, c, pt: (0, 0))
    rep_spec = pl.BlockSpec((1, rows, LANES), lambda i, c, pt: (i, 0, 0))
    head_spec = pl.BlockSpec((1, DIFF_HEADS, LANES), lambda i, c, pt: (i, 0, 0))
    return pl.pallas_call(
        functools.partial(_dec_diff_kernel, layer=layer, n=n, lam_init=lam_init),
        grid_spec=pltpu.PrefetchScalarGridSpec(
            num_scalar_prefetch=1,
            grid=(nb, n_pages // n),
            in_specs=[full(lamv), full(g_sub), rep_spec, rep_spec, head_spec,
                      pl.BlockSpec(memory_space=pl.ANY), pl.BlockSpec(memory_space=pl.ANY)],
            out_specs=head_spec,
            scratch_shapes=[pltpu.VMEM((2, n, rows, page), F32),
                            pltpu.VMEM((2, n, vc.shape[2], vc.shape[3]), F32),
                            pltpu.SemaphoreType.DMA((2, 2)),
                            pltpu.VMEM((n_map, LANES), F32), pltpu.VMEM((n_map, LANES), F32),
                            pltpu.VMEM((DIFF_HEADS, n_map, LANES), F32)],
        ),
        out_shape=jax.ShapeDtypeStruct((nb, DIFF_HEADS, LANES), F32),
        compiler_params=_cparams(("arbitrary", "arbitrary")),
        name=f"dec_diff_l{layer}",
    )(page_table, lamv, g_sub, qrep, knrep, vnew, kc, vc)


def _rope_tables(pos):
    inv = ROPE_THETA ** (-jnp.arange(HALF_DIM, dtype=F32) / HALF_DIM)
    ang = pos.astype(F32)[:, None] * inv[None, :]
    cos, sin = jnp.cos(ang), jnp.sin(ang)
    reps = LANES // HEAD_DIM
    cos_r = jnp.tile(jnp.concatenate([cos, cos], axis=1), (1, reps))
    sin_r = jnp.tile(jnp.concatenate([-sin, sin], axis=1), (1, reps))
    return cos_r, sin_r, cos.T, sin.T


def _lane_rep(a):
    return jnp.broadcast_to(a[:, :, None], a.shape + (LANES,))


def _pages_per_chunk(n_pages, page):
    per_blk = MOBA_BLOCK // page
    for n in (16, 8, 4, 2):
        if n_pages % n == 0 and n % per_blk == 0:
            return n
    raise ValueError("page count must be a multiple of the pages per MoBA block")


def kernel(x_prompt, x_sample, cache_k_moba, cache_v_moba, cache_k_diff, cache_v_diff, page_table, c_prompt, c_sample, w_in, w_out, g_attn, g_mlp, w_ada, b_ada, w_up, w_down, lambda_q1, lambda_k1, lambda_q2, lambda_k2, g_subln, g_final):
    b, t, d = x_prompt.shape
    nb, dec_seq, _ = x_sample.shape
    depth = w_in.shape[0]
    n_pool, page = cache_k_moba.shape[1], cache_k_moba.shape[2]
    n_pages = page_table.shape[1]
    assert d == D_MODEL and dec_seq == 1 and t % MOBA_BLOCK == 0
    assert MOBA_BLOCK % page == 0 and page == LANES and n_pages * page // MOBA_BLOCK >= MOBA_TOPK
    n_chunk_pages = _pages_per_chunk(n_pages, page)

    to_kt = lambda c: jnp.transpose(c, (0, 1, 3, 4, 2)).reshape(depth, n_pool, GROUP_W, page)
    kc_m, vc_m, kc_d = to_kt(cache_k_moba), to_kt(cache_v_moba), to_kt(cache_k_diff)
    vc_d = cache_v_diff.reshape(depth, n_pool, page * DIFF_HEADS, 2 * HEAD_DIM)

    mod = _ada_call(jnp.concatenate([c_prompt, c_sample], axis=0), w_ada, b_ada)
    cos_r, sin_r, cos_t, sin_t = _rope_tables(jnp.arange(t, dtype=jnp.int32))
    cos_s, sin_s, _, _ = _rope_tables(jnp.full((nb,), n_pages * page, jnp.int32))

    xp = x_prompt
    xs = x_sample.reshape(1, nb, d)
    kt_zero = jnp.zeros((depth, b, GROUP_W, t), F32)
    layered = (jnp.zeros((depth, b, DIFF_HEADS * t, LANES), F32), kt_zero, kt_zero, kt_zero)
    sample_new = []
    for l in range(depth):
        lam_init = 0.8 - 0.6 * math.exp(-0.3 * l)
        final = l == depth - 1
        lamv = jnp.stack([lambda_q1[l], lambda_k1[l], lambda_q2[l], lambda_k2[l]])
        g_sub = g_subln[l].reshape(1, -1)
        g_a, g_m = g_attn[l].reshape(1, d), g_mlp[l].reshape(1, d)
        g_f = g_final.reshape(1, d)
        mods_p = [m.reshape(b, 1, d) for m in jnp.split(mod[l, :b], 6, axis=-1)]
        mods_s = [m.reshape(1, nb, d) for m in jnp.split(mod[l, b:], 6, axis=-1)]
        win = w_in[l]
        groups = [win[:, k * GROUP_W:(k + 1) * GROUP_W] for k in range(6)]
        wa = jnp.concatenate([groups[0], groups[3], groups[5]], axis=1).astype(BF16)
        wbt = jnp.concatenate([groups[1], groups[2], groups[4]], axis=1).T.astype(BF16)

        sh_a, sc_a, gt_a, sh_m, sc_m, gt_m = mods_p
        qm, qd, vd_all, ktm_all, vtm_all, ktd_all = _proj_call(
            l, depth, xp, g_a, sc_a, sh_a, wa, wbt, cos_r, sin_r, cos_t, sin_t, layered)
        layered = (vd_all, ktm_all, vtm_all, ktd_all)
        om = _moba_call(l, qm, ktm_all, vtm_all)
        od = _diff_call(l, lam_init, lamv, g_sub, qd, ktd_all, vd_all)
        x1, h2 = _outproj_call(f"outproj_l{l}", xp, om, od, w_out[l].astype(BF16), gt_a, g_m, sc_m, sh_m,
                               min(PROJ_TM, t), False)
        xp = _mlp_call(f"mlp_l{l}", h2, x1, w_up[l].astype(BF16), w_down[l].astype(BF16), gt_m, g_f,
                       min(MLP_TM, t), False, final)

        sh_a, sc_a, gt_a, sh_m, sc_m, gt_m = mods_s
        ps = _sproj_call(l, xs[0], g_a, sc_a[0], sh_a[0], win, cos_s, sin_s)
        qm_s, km_s, vm_s, qd_s, kd_s, vd_s = [ps[:, k * GROUP_W:(k + 1) * GROUP_W] for k in range(6)]
        sample_new.append((km_s, vm_s, kd_s, vd_s))
        qrep_m = _lane_rep(qm_s)
        sel = _dec_gate_call(l, page_table, qrep_m, kc_m, n_chunk_pages)[:, :, :, 0]
        om_s = _dec_moba_call(l, page_table, sel.reshape(nb, MOBA_TOPK * MOBA_HEADS), qrep_m, _lane_rep(km_s),
                              _lane_rep(vm_s), kc_m, vc_m)[:, :, 0]
        od_s = _dec_diff_call(l, lam_init, page_table, lamv, g_sub, _lane_rep(qd_s), _lane_rep(kd_s),
                              vd_s.reshape(nb, DIFF_HEADS, 2 * HEAD_DIM), kc_d, vc_d, n_chunk_pages)
        x1s, h2s = _outproj_call(f"outproj_s_l{l}", xs, om_s.reshape(1, nb, GROUP_W), od_s.reshape(1, nb, GROUP_W),
                                 w_out[l], gt_a, g_m, sc_m, sh_m, nb, True)
        xs = _mlp_call(f"mlp_s_l{l}", h2s, x1s, w_up[l], w_down[l], gt_m, g_f, nb, True, final)

    vd_all, ktm_all, vtm_all, ktd_all = layered
    from_kt = lambda a: jnp.transpose(a.reshape(depth, b, GROUP_W // HEAD_DIM, HEAD_DIM, t), (0, 1, 4, 2, 3))
    stack_s = lambda k, shape: jnp.stack([s[k] for s in sample_new]).reshape((depth, nb, 1) + shape)
    return (
        xp,
        xs.reshape(nb, 1, d),
        from_kt(ktm_all), from_kt(vtm_all), from_kt(ktd_all),
        vd_all.reshape(depth, b, t, DIFF_HEADS, 2 * HEAD_DIM),
        stack_s(0, (MOBA_HEADS, HEAD_DIM)), stack_s(1, (MOBA_HEADS, HEAD_DIM)),
        stack_s(2, (2 * DIFF_HEADS, HEAD_DIM)), stack_s(3, (DIFF_HEADS, 2 * HEAD_DIM)),
    )
```

```python
import functools
import math

import jax
import jax.numpy as jnp
from jax import lax
from jax.experimental import pallas as pl
from jax.experimental.pallas import tpu as pltpu

F32 = jnp.float32
BF16 = jnp.bfloat16
HIGHEST = lax.Precision.HIGHEST

D_MODEL = 1024
HEAD_DIM = 64
HALF_DIM = HEAD_DIM // 2
MOBA_HEADS = 8
DIFF_HEADS = 4
GROUP_W = 512
MOBA_BLOCK = 256
MOBA_TOPK = 3
D_FF = 4 * D_MODEL
ROPE_THETA = 10000.0
EPS = 1e-6
NEG = -1e30
Q_SCALE = 1.0 / math.sqrt(HEAD_DIM)

LANES = 128
SUBLANES = 8
BF16_ROWS = 16
VMEM_LIMIT_BYTES = 52 * 1024 * 1024

PROJ_TM = 512
MLP_TM = 1024
MLP_TF = 1024
ADA_TN = 1536
DEC_GROUP_PAGES = 4


def _cparams(semantics):
    return pltpu.CompilerParams(dimension_semantics=semantics, vmem_limit_bytes=VMEM_LIMIT_BYTES)


def _rmsnorm(x, g):
    return x * lax.rsqrt(jnp.mean(x * x, axis=-1, keepdims=True) + EPS) * g


def _dot(a, b, exact):
    if exact:
        return jnp.dot(a.astype(F32), b.astype(F32), precision=HIGHEST, preferred_element_type=F32)
    return jnp.dot(a.astype(BF16), b.astype(BF16), preferred_element_type=F32)


def _dot_nt(a, b):
    return lax.dot_general(a, b, (((1,), (1,)), ((), ())), preferred_element_type=F32)


def _swap_halves(x):
    lane = lax.broadcasted_iota(jnp.int32, x.shape, 1)
    first_half = (lane % HEAD_DIM) < HALF_DIM
    return jnp.where(first_half, pltpu.roll(x, LANES - HALF_DIM, 1), pltpu.roll(x, HALF_DIM, 1))


def _rope_rows(p, cos, sin_signed):
    return p * cos + _swap_halves(p) * sin_signed


def _ada_kernel(c_ref, w_ref, b_ref, o_ref):
    c = c_ref[...]
    silu = c / (1.0 + jnp.exp(-c))
    o_ref[0] = _dot(silu, w_ref[0], True) + b_ref[0]


def _ada_call(c_all, w_ada, b_ada):
    depth, d, n = w_ada.shape
    rows = c_all.shape[0]
    return pl.pallas_call(
        _ada_kernel,
        grid=(depth, n // ADA_TN),
        in_specs=[
            pl.BlockSpec((rows, d), lambda l, j: (0, 0)),
            pl.BlockSpec((1, d, ADA_TN), lambda l, j: (l, 0, j)),
            pl.BlockSpec((1, 1, ADA_TN), lambda l, j: (l, 0, j)),
        ],
        out_specs=pl.BlockSpec((1, rows, ADA_TN), lambda l, j: (l, 0, j)),
        out_shape=jax.ShapeDtypeStruct((depth, rows, n), F32),
        compiler_params=_cparams(("parallel", "parallel")),
        name="adaln",
    )(c_all, w_ada, b_ada.reshape(depth, 1, n))


def _proj_kernel(x_ref, g_ref, sc_ref, sh_ref, wa_ref, wbt_ref, ct_ref, st_ref, *rest):
    qmt_ref, qdt_ref, vd_ref, ktm_ref, vtm_ref, ktd_ref = rest[-6:]
    tm = x_ref.shape[1]
    h = (_rmsnorm(x_ref[0], g_ref[...]) * (1.0 + sc_ref[0]) + sh_ref[0]).astype(BF16)

    pa = jnp.dot(h, wa_ref[...], preferred_element_type=F32)
    for hd in range(DIFF_HEADS):
        vd_ref[0, 0, pl.ds(hd, tm, stride=DIFF_HEADS), :] = pa[:, hd * LANES:(hd + 1) * LANES]

    pb = _dot_nt(wbt_ref[...], h)
    cos_t = ct_ref[...]
    sin_t = st_ref[...]
    vtm_ref[0, 0] = pb[2 * GROUP_W:3 * GROUP_W]
    for src, dst, scale in ((0, qmt_ref, Q_SCALE), (GROUP_W, ktm_ref, None),
                            (3 * GROUP_W, qdt_ref, Q_SCALE), (4 * GROUP_W, ktd_ref, None)):
        for hd in range(GROUP_W // HEAD_DIM):
            r0 = src + hd * HEAD_DIM
            x1 = pb[r0:r0 + HALF_DIM]
            x2 = pb[r0 + HALF_DIM:r0 + HEAD_DIM]
            y1 = x1 * cos_t - x2 * sin_t
            y2 = x1 * sin_t + x2 * cos_t
            o0 = hd * HEAD_DIM
            if scale is None:
                dst[0, 0, o0:o0 + HALF_DIM, :] = y1
                dst[0, 0, o0 + HALF_DIM:o0 + HEAD_DIM, :] = y2
            else:
                dst[0, o0:o0 + HALF_DIM, :] = (y1 * scale).astype(dst.dtype)
                dst[0, o0 + HALF_DIM:o0 + HEAD_DIM, :] = (y2 * scale).astype(dst.dtype)


def _proj_call(layer, depth, x, g, sc, sh, wa, wbt, cos_t, sin_t, layered):
    b, t, d = x.shape
    tm = min(PROJ_TM, t)
    in_specs = [
        pl.BlockSpec((1, tm, d), lambda i, j: (i, j, 0)),
        pl.BlockSpec((1, d), lambda i, j: (0, 0)),
        pl.BlockSpec((1, 1, d), lambda i, j: (i, 0, 0)),
        pl.BlockSpec((1, 1, d), lambda i, j: (i, 0, 0)),
        pl.BlockSpec(wa.shape, lambda i, j: (0, 0)),
        pl.BlockSpec(wbt.shape, lambda i, j: (0, 0)),
        pl.BlockSpec((HALF_DIM, tm), lambda i, j: (0, j)),
        pl.BlockSpec((HALF_DIM, tm), lambda i, j: (0, j)),
    ]
    args = [x, g, sc, sh, wa, wbt, cos_t, sin_t]
    aliases = {}
    for k, arr in enumerate(layered):
        in_specs.append(pl.BlockSpec(memory_space=pl.ANY))
        aliases[len(args)] = 2 + k
        args.append(arr)
    kt_shape = jax.ShapeDtypeStruct((depth, b, GROUP_W, t), F32)
    kt_spec = pl.BlockSpec((1, 1, GROUP_W, tm), lambda i, j: (layer, i, 0, j))
    qt_spec = pl.BlockSpec((1, GROUP_W, tm), lambda i, j: (i, 0, j))
    return pl.pallas_call(
        _proj_kernel,
        grid=(b, t // tm),
        in_specs=in_specs,
        out_specs=[
            qt_spec, qt_spec,
            pl.BlockSpec((1, 1, DIFF_HEADS * tm, LANES), lambda i, j: (layer, i, j, 0)),
            kt_spec, kt_spec, kt_spec,
        ],
        out_shape=[
            jax.ShapeDtypeStruct((b, GROUP_W, t), BF16),
            jax.ShapeDtypeStruct((b, GROUP_W, t), BF16),
            jax.ShapeDtypeStruct((depth, b, DIFF_HEADS * t, LANES), F32),
            kt_shape, kt_shape, kt_shape,
        ],
        input_output_aliases=aliases,
        compiler_params=_cparams(("parallel", "parallel")),
        name=f"proj_l{layer}",
    )(*args)


def _tile_rows(row, n_rows):
    return jnp.concatenate([row] * (n_rows // SUBLANES), axis=0)


def _stash_scores(scores, slot, s_s, mc_s):
    for idx, s in enumerate(scores):
        s_s[slot, idx] = s
        mc_s[slot, idx] = jnp.broadcast_to(jnp.max(s, axis=0, keepdims=True), mc_s.shape[2:])


def _fold_scores(slot, values, s_s, mc_s, m_s, l_s, acc_s):
    for idx, vt in enumerate(values):
        s = s_s[slot, idx]
        m_prev = m_s[idx]
        m_next = jnp.maximum(m_prev, mc_s[slot, idx])
        alpha = jnp.exp(m_prev - m_next)
        p = jnp.exp(s - _tile_rows(m_next, s.shape[0]))
        l_s[idx] = alpha * l_s[idx] + jnp.broadcast_to(jnp.sum(p, axis=0, keepdims=True), alpha.shape)
        acc_s[idx] = (acc_s[idx] * _tile_rows(alpha, vt.shape[0])
                      + jnp.dot(vt, p.astype(BF16), preferred_element_type=F32))
        m_s[idx] = m_next


def _attend(last, scores_fn, values_fn, s_s, mc_s, m_s, l_s, acc_s):
    m_s[...] = jnp.full(m_s.shape, -jnp.inf, F32)
    l_s[...] = jnp.zeros_like(l_s)
    acc_s[...] = jnp.zeros_like(acc_s)
    _stash_scores(scores_fn(last, True), 0, s_s, mc_s)

    def body(it, carry):
        slot = lax.rem(it, 2)
        ahead = scores_fn(it, False)
        _fold_scores(slot, values_fn(jnp.where(it == 0, last, it - 1)), s_s, mc_s, m_s, l_s, acc_s)
        _stash_scores(ahead, 1 - slot, s_s, mc_s)
        return carry

    lax.fori_loop(0, last, body, 0)
    _fold_scores(lax.rem(last, 2), values_fn(jnp.maximum(last - 1, 0)), s_s, mc_s, m_s, l_s, acc_s)


def _split_heads(qt):
    zero = jnp.zeros((HEAD_DIM, qt.shape[1]), qt.dtype)
    return (jnp.concatenate([qt[:HEAD_DIM], zero], axis=0), jnp.concatenate([zero, qt[HEAD_DIM:]], axis=0))


def _causal_pair(pair, qi, n_keys, tq):
    key = pair * n_keys + lax.broadcasted_iota(jnp.int32, (n_keys, tq), 0)
    qry = qi * tq + lax.broadcasted_iota(jnp.int32, (n_keys, tq), 1)
    return key <= qry


def _top_blocks_t(gate, qi, n_blk):
    blk = lax.broadcasted_iota(jnp.int32, gate.shape, 0)
    g = jnp.where(blk < qi, gate, -jnp.inf)
    rank = jnp.zeros(gate.shape, F32)
    for m in range(n_blk):
        gm = jnp.broadcast_to(g[m:m + 1, :], gate.shape)
        beats = (gm > g) | ((gm == g) & (m < blk))
        rank = rank + jnp.where(beats, 1.0, 0.0)
    keep = ((blk < qi) & (rank < float(MOBA_TOPK))) | (blk == qi)
    return jnp.where(keep, 1.0, 0.0)


def _moba_kernel(qt_ref, kt_ref, vt_ref, o_ref, k_s, vt_s, kmh_s, kml_s, sel_s, s_s, mc_s, m_s, l_s, acc_s):
    qi = pl.program_id(2)
    n_pair, n_keys, _ = k_s.shape
    blk = n_keys // 2
    n_blk = 2 * n_pair
    tq = qt_ref.shape[2]

    @pl.when(qi == 0)
    def _stage():
        row = lax.broadcasted_iota(jnp.int32, kmh_s.shape, 0)
        km = jnp.zeros(kmh_s.shape, F32)
        for n in range(n_blk):
            k = kt_ref[0, 0, :, n * blk:(n + 1) * blk].T
            k_s[n // 2, (n % 2) * blk:(n % 2 + 1) * blk, :] = k.astype(BF16)
            vt_s[n // 2, :, (n % 2) * blk:(n % 2 + 1) * blk] = vt_ref[0, 0, :, n * blk:(n + 1) * blk].astype(BF16)
            km = jnp.where(row == n, jnp.sum(k, axis=0, keepdims=True) * (1.0 / blk), km)
        hi = km.astype(BF16)
        kmh_s[...] = hi
        kml_s[...] = (km - hi.astype(F32)).astype(BF16)

    q_heads = _split_heads(qt_ref[0])
    for idx, qh in enumerate(q_heads):
        gate = (jnp.dot(kmh_s[...], qh, preferred_element_type=F32)
                + jnp.dot(kml_s[...], qh, preferred_element_type=F32))
        sel_s[idx] = _top_blocks_t(gate, qi, n_blk)

    def scores(pair, with_diagonal):
        out = []
        for idx, qh in enumerate(q_heads):
            s = jnp.dot(k_s[pair], qh, preferred_element_type=F32)
            keep = jnp.concatenate([
                jnp.broadcast_to(sel_s[idx, pl.ds(2 * pair, 1), :], (blk, tq)),
                jnp.broadcast_to(sel_s[idx, pl.ds(2 * pair + 1, 1), :], (blk, tq))], axis=0) > 0.0
            s = jnp.where(keep, s, NEG)
            if with_diagonal:
                s = jnp.where(_causal_pair(pair, qi, n_keys, tq), s, NEG)
            out.append(s)
        return out

    def values(pair):
        return [vt_s[pair, idx * HEAD_DIM:(idx + 1) * HEAD_DIM, :] for idx in range(2)]

    _attend(qi // 2, scores, values, s_s, mc_s, m_s, l_s, acc_s)
    ot = jnp.concatenate([acc_s[idx] / _tile_rows(l_s[idx], HEAD_DIM) for idx in range(2)], axis=0)
    o_ref[0] = ot.T.astype(o_ref.dtype)


def _attn_scratch(n_pair, n_keys, tq, v_rows):
    return [
        pltpu.VMEM((n_pair, n_keys, LANES), BF16),
        pltpu.VMEM((n_pair, LANES, n_keys), BF16),
        pltpu.VMEM((2, 2, n_keys, tq), F32),
        pltpu.VMEM((2, 2, SUBLANES, tq), F32),
        pltpu.VMEM((2, SUBLANES, tq), F32),
        pltpu.VMEM((2, SUBLANES, tq), F32),
        pltpu.VMEM((2, v_rows, tq), F32),
    ]


def _moba_call(layer, qt, kt_all, vt_all):
    b, _, t = qt.shape
    blk = MOBA_BLOCK
    n_blk = t // blk
    kv_spec = pl.BlockSpec((1, 1, LANES, t), lambda i, h, j: (layer, i, h, 0))
    k_s, vt_s, s_s, mc_s, m_s, l_s, acc_s = _attn_scratch(n_blk // 2, 2 * blk, blk, HEAD_DIM)
    mean_rows = -(-n_blk // BF16_ROWS) * BF16_ROWS
    return pl.pallas_call(
        _moba_kernel,
        grid=(b, GROUP_W // LANES, n_blk),
        in_specs=[pl.BlockSpec((1, LANES, blk), lambda i, h, j: (i, h, j)), kv_spec, kv_spec],
        out_specs=pl.BlockSpec((1, blk, LANES), lambda i, h, j: (i, j, h)),
        out_shape=jax.ShapeDtypeStruct((b, t, GROUP_W), BF16),
        scratch_shapes=[
            k_s, vt_s,
            pltpu.VMEM((mean_rows, LANES), BF16),
            pltpu.VMEM((mean_rows, LANES), BF16),
            pltpu.VMEM((2, mean_rows, blk), F32),
            s_s, mc_s, m_s, l_s, acc_s,
        ],
        compiler_params=_cparams(("parallel", "parallel", "arbitrary")),
        name=f"moba_l{layer}",
    )(qt, kt_all, vt_all)


def _lambda(lam_ref, lam_init):
    a = jnp.sum(lam_ref[0:1, :] * lam_ref[1:2, :], axis=1, keepdims=True)
    b = jnp.sum(lam_ref[2:3, :] * lam_ref[3:4, :], axis=1, keepdims=True)
    return jnp.exp(a) - jnp.exp(b) + lam_init


def _diff_kernel(lam_ref, g_ref, qt_ref, kt_ref, v_ref, o_ref, k_s, vt_s, s_s, mc_s, m_s, l_s, acc_s, *, lam_init):
    hd = pl.program_id(1)
    qi = pl.program_id(2)
    n_pair, n_keys, _ = k_s.shape
    blk = n_keys // 2
    tq = qt_ref.shape[2]

    @pl.when(qi == 0)
    def _stage():
        for n in range(2 * n_pair):
            k = kt_ref[0, 0, :, n * blk:(n + 1) * blk].T
            k_s[n // 2, (n % 2) * blk:(n % 2 + 1) * blk, :] = k.astype(BF16)
            v = v_ref[0, 0, pl.ds(n * blk * DIFF_HEADS + hd, blk, stride=DIFF_HEADS), :]
            vt_s[n // 2, :, (n % 2) * blk:(n % 2 + 1) * blk] = v.T.astype(BF16)

    q_maps = _split_heads(qt_ref[0])

    def scores(pair, with_diagonal):
        out = [jnp.dot(k_s[pair], qm, preferred_element_type=F32) for qm in q_maps]
        if with_diagonal:
            causal = _causal_pair(pair, qi, n_keys, tq)
            out = [jnp.where(causal, s, NEG) for s in out]
        return out

    def values(pair):
        vt = vt_s[pair]
        return [vt, vt]

    _attend(qi // 2, scores, values, s_s, mc_s, m_s, l_s, acc_s)
    o1, o2 = [acc_s[idx] / _tile_rows(l_s[idx], LANES) for idx in range(2)]
    ot = o1 - _lambda(lam_ref, lam_init) * o2
    ot = ot * lax.rsqrt(jnp.mean(ot * ot, axis=0, keepdims=True) + EPS)
    o_ref[0] = (ot.T * g_ref[...] * (1.0 - lam_init)).astype(o_ref.dtype)


def _diff_call(layer, lam_init, lamv, g_sub, qt, kt_all, v_all):
    b, _, t = qt.shape
    blk = MOBA_BLOCK
    n_blk = t // blk
    return pl.pallas_call(
        functools.partial(_diff_kernel, lam_init=lam_init),
        grid=(b, DIFF_HEADS, n_blk),
        in_specs=[
            pl.BlockSpec(lamv.shape, lambda i, h, j: (0, 0)),
            pl.BlockSpec(g_sub.shape, lambda i, h, j: (0, 0)),
            pl.BlockSpec((1, LANES, blk), lambda i, h, j: (i, h, j)),
            pl.BlockSpec((1, 1, LANES, t), lambda i, h, j: (layer, i, h, 0)),
            pl.BlockSpec((1, 1, DIFF_HEADS * t, LANES), lambda i, h, j: (layer, i, 0, 0)),
        ],
        out_specs=pl.BlockSpec((1, blk, LANES), lambda i, h, j: (i, j, h)),
        out_shape=jax.ShapeDtypeStruct((b, t, GROUP_W), BF16),
        scratch_shapes=_attn_scratch(n_blk // 2, 2 * blk, blk, 2 * HEAD_DIM),
        compiler_params=_cparams(("parallel", "parallel", "arbitrary")),
        name=f"diff_l{layer}",
    )(lamv, g_sub, qt, kt_all, v_all)


def _outproj_kernel(x_ref, om_ref, od_ref, w_ref, gt_ref, g_ref, sc_ref, sh_ref, x1_ref, h2_ref, *, exact):
    o = _dot(om_ref[0], w_ref[0:GROUP_W, :], exact) + _dot(od_ref[0], w_ref[GROUP_W:2 * GROUP_W, :], exact)
    x1 = x_ref[0] + gt_ref[0] * o
    x1_ref[0] = x1
    h2_ref[0] = (_rmsnorm(x1, g_ref[...]) * (1.0 + sc_ref[0]) + sh_ref[0]).astype(h2_ref.dtype)


def _mod_spec(mod, tm):
    d = mod.shape[-1]
    if mod.shape[1] == 1:
        return pl.BlockSpec((1, 1, d), lambda i, j, *_: (i, 0, 0))
    return pl.BlockSpec((1, tm, d), lambda i, j, *_: (i, j, 0))


def _outproj_call(name, x, om, od, w, gt, g, sc, sh, tm, exact):
    grp, rows, d = x.shape
    row_spec = lambda width: pl.BlockSpec((1, tm, width), lambda i, j: (i, j, 0))
    return pl.pallas_call(
        functools.partial(_outproj_kernel, exact=exact),
        grid=(grp, rows // tm),
        in_specs=[
            row_spec(d), row_spec(GROUP_W), row_spec(GROUP_W),
            pl.BlockSpec(w.shape, lambda i, j: (0, 0)),
            _mod_spec(gt, tm),
            pl.BlockSpec((1, d), lambda i, j: (0, 0)),
            _mod_spec(sc, tm), _mod_spec(sh, tm),
        ],
        out_specs=[row_spec(d), row_spec(d)],
        out_shape=[jax.ShapeDtypeStruct(x.shape, F32), jax.ShapeDtypeStruct(x.shape, F32 if exact else BF16)],
        compiler_params=_cparams(("parallel", "parallel")),
        name=name,
    )(x, om, od, w, gt, g, sc, sh)


def _mlp_kernel(h_ref, x1_ref, wu_ref, wd_ref, gt_ref, gf_ref, o_ref, acc_s, *, exact, final_norm):
    f = pl.program_id(2)

    @pl.when(f == 0)
    def _zero():
        acc_s[...] = jnp.zeros_like(acc_s)

    u = jnp.maximum(_dot(h_ref[0], wu_ref[...], exact), 0.0)
    acc_s[...] += _dot(u * u, wd_ref[...], exact)

    @pl.when(f == pl.num_programs(2) - 1)
    def _finish():
        x2 = x1_ref[0] + gt_ref[0] * acc_s[...]
        o_ref[0] = _rmsnorm(x2, gf_ref[...]) if final_norm else x2


def _mlp_call(name, h, x1, wu, wd, gt, gf, tm, exact, final_norm):
    grp, rows, d = x1.shape
    ff = wu.shape[1]
    tf = min(MLP_TF, ff)
    row_spec = pl.BlockSpec((1, tm, d), lambda i, j, f: (i, j, 0))
    return pl.pallas_call(
        functools.partial(_mlp_kernel, exact=exact, final_norm=final_norm),
        grid=(grp, rows // tm, ff // tf),
        in_specs=[
            row_spec, row_spec,
            pl.BlockSpec((d, tf), lambda i, j, f: (0, f)),
            pl.BlockSpec((tf, d), lambda i, j, f: (f, 0)),
            _mod_spec(gt, tm),
            pl.BlockSpec((1, d), lambda i, j, f: (0, 0)),
        ],
        out_specs=row_spec,
        out_shape=jax.ShapeDtypeStruct(x1.shape, F32),
        scratch_shapes=[pltpu.VMEM((tm, d), F32)],
        compiler_params=_cparams(("parallel", "parallel", "arbitrary")),
        name=name,
    )(h, x1, wu, wd, gt, gf)


def _sproj_kernel(x_ref, g_ref, sc_ref, sh_ref, w_ref, cr_ref, sr_ref, o_ref):
    j = pl.program_id(0)
    h = _rmsnorm(x_ref[...], g_ref[...]) * (1.0 + sc_ref[...]) + sh_ref[...]
    p = _dot(h, w_ref[...], True)
    is_q = (j == 0) | (j == 3)
    is_rope = is_q | (j == 1) | (j == 4)
    scale = jnp.where(is_q, Q_SCALE, 1.0)
    cos_r = cr_ref[...]
    sin_r = sr_ref[...]
    for c in range(GROUP_W // LANES):
        pc = p[:, c * LANES:(c + 1) * LANES]
        o_ref[:, c * LANES:(c + 1) * LANES] = jnp.where(is_rope, _rope_rows(pc, cos_r, sin_r), pc) * scale


def _sproj_call(layer, x, g, sc, sh, w, cos_r, sin_r):
    rows, d = x.shape
    n = w.shape[1]
    full = lambda a: pl.BlockSpec(a.shape, lambda j: (0, 0))
    return pl.pallas_call(
        _sproj_kernel,
        grid=(n // GROUP_W,),
        in_specs=[full(x), full(g), full(sc), full(sh), pl.BlockSpec((d, GROUP_W), lambda j: (0, j)),
                  full(cos_r), full(sin_r)],
        out_specs=pl.BlockSpec((rows, GROUP_W), lambda j: (0, j)),
        out_shape=jax.ShapeDtypeStruct((rows, n), F32),
        compiler_params=_cparams(("parallel",)),
        name=f"sproj_l{layer}",
    )(x, g, sc, sh, w, cos_r, sin_r)


def _head_scores(kt, qrep):
    n_heads = kt.shape[0] // HEAD_DIM
    prod = (kt * qrep).reshape(n_heads, HEAD_DIM // SUBLANES, SUBLANES, LANES)
    return jnp.sum(jnp.sum(prod, axis=1), axis=1)


def _chunk_copies(pt_ref, srcs, bufs, sem, layer, seq, chunk, slot, n):
    out = []
    for k, (src, buf) in enumerate(zip(srcs, bufs)):
        for i in range(n):
            page = pt_ref[seq, chunk * n + i]
            out.append(pltpu.make_async_copy(src.at[layer, page], buf.at[slot, i], sem.at[k, slot]))
    return out


def _stream_chunks(pt_ref, srcs, bufs, sem, layer, n):
    seq = pl.program_id(0)
    chunk = pl.program_id(1)
    n_chunk = pl.num_programs(1)
    step = seq * n_chunk + chunk
    slot = lax.rem(step, 2)

    @pl.when(step == 0)
    def _prime():
        for cp in _chunk_copies(pt_ref, srcs, bufs, sem, layer, 0, 0, 0, n):
            cp.start()

    @pl.when(step + 1 < pl.num_programs(0) * n_chunk)
    def _prefetch():
        wrap = chunk + 1 == n_chunk
        nseq = jnp.where(wrap, seq + 1, seq)
        nchunk = jnp.where(wrap, 0, chunk + 1)
        for cp in _chunk_copies(pt_ref, srcs, bufs, sem, layer, nseq, nchunk, 1 - slot, n):
            cp.start()

    for cp in _chunk_copies(pt_ref, srcs, bufs, sem, layer, seq, chunk, slot, n):
        cp.wait()
    return slot


def _dec_gate_kernel(pt_ref, qrep_ref, kc_ref, sel_ref, buf, sem, g_s, *, layer, n):
    chunk = pl.program_id(1)
    slot = _stream_chunks(pt_ref, (kc_ref,), (buf,), sem, layer, n)
    qrep = qrep_ref[0]
    per_blk = MOBA_BLOCK // buf.shape[3]
    for k in range(n // per_blk):
        pages = buf[slot, per_blk * k]
        for extra in range(1, per_blk):
            pages = pages + buf[slot, per_blk * k + extra]
        g_s[chunk * (n // per_blk) + k] = _head_scores(pages, qrep)

    @pl.when(chunk == pl.num_programs(1) - 1)
    def _select():
        n_blk = g_s.shape[0]
        flat = g_s[...].reshape(n_blk * MOBA_HEADS, LANES)
        tot = jnp.broadcast_to(jnp.sum(flat, axis=1, keepdims=True), flat.shape).reshape(g_s.shape)
        blk = lax.broadcasted_iota(jnp.int32, g_s.shape, 0).astype(F32)
        for r in range(MOBA_TOPK):
            m = jnp.max(tot, axis=0, keepdims=True)
            first = jnp.min(jnp.where(tot == m, blk, float(n_blk)), axis=0, keepdims=True)
            sel_ref[0, r] = first[0].astype(jnp.int32)
            tot = jnp.where(blk == first, -jnp.inf, tot)


def _dec_gate_call(layer, page_table, qrep, kc, n):
    nb, n_pages = page_table.shape
    rows, page = kc.shape[2], kc.shape[3]
    n_blk = n_pages * page // MOBA_BLOCK
    return pl.pallas_call(
        functools.partial(_dec_gate_kernel, layer=layer, n=n),
        grid_spec=pltpu.PrefetchScalarGridSpec(
            num_scalar_prefetch=1,
            grid=(nb, n_pages // n),
            in_specs=[pl.BlockSpec((1, rows, LANES), lambda i, c, pt: (i, 0, 0)),
                      pl.BlockSpec(memory_space=pl.ANY)],
            out_specs=pl.BlockSpec((1, MOBA_TOPK, MOBA_HEADS, LANES), lambda i, c, pt: (i, 0, 0, 0)),
            scratch_shapes=[pltpu.VMEM((2, n, rows, page), F32), pltpu.SemaphoreType.DMA((1, 2)),
                            pltpu.VMEM((n_blk, MOBA_HEADS, LANES), F32)],
        ),
        out_shape=jax.ShapeDtypeStruct((nb, MOBA_TOPK, MOBA_HEADS, LANES), jnp.int32),
        compiler_params=_cparams(("arbitrary", "arbitrary")),
        name=f"dec_gate_l{layer}",
    )(page_table, qrep, kc)


def _dec_moba_kernel(pt_ref, sel_ref, qrep_ref, knrep_ref, vnrep_ref, kc_ref, vc_ref, o_ref, kbuf, vbuf, sem,
                     *, layer):
    seq = pl.program_id(0)
    slot = lax.rem(seq, 2)
    per_blk = MOBA_BLOCK // kbuf.shape[3]
    per_head = MOBA_TOPK * per_blk

    def copies(s, sl):
        out = []
        for hd in range(MOBA_HEADS):
            rows = pl.ds(hd * HEAD_DIM, HEAD_DIM)
            for r in range(MOBA_TOPK):
                blk = sel_ref[s, r * MOBA_HEADS + hd]
                for pg in range(per_blk):
                    page = pt_ref[s, per_blk * blk + pg]
                    j = hd * per_head + r * per_blk + pg
                    out.append(pltpu.make_async_copy(kc_ref.at[layer, page, rows], kbuf.at[sl, j], sem.at[0, sl]))
                    out.append(pltpu.make_async_copy(vc_ref.at[layer, page, rows], vbuf.at[sl, j], sem.at[1, sl]))
        return out

    @pl.when(seq == 0)
    def _prime():
        for cp in copies(0, 0):
            cp.start()

    @pl.when(seq + 1 < pl.num_programs(0))
    def _prefetch():
        for cp in copies(seq + 1, 1 - slot):
            cp.start()

    for cp in copies(seq, slot):
        cp.wait()

    for hd in range(MOBA_HEADS):
        r0 = hd * HEAD_DIM
        q = qrep_ref[0, r0:r0 + HEAD_DIM, :]
        s_self = jnp.sum(q * knrep_ref[0, r0:r0 + HEAD_DIM, :], axis=0, keepdims=True)
        s_pages = [jnp.sum(kbuf[slot, hd * per_head + j] * q, axis=0, keepdims=True) for j in range(per_head)]
        m = s_self
        for s in s_pages:
            m = jnp.maximum(m, jnp.max(s, axis=1, keepdims=True))
        p_self = jnp.exp(s_self - m)
        l = p_self
        o = vnrep_ref[0, r0:r0 + HEAD_DIM, :] * p_self
        for j, s in enumerate(s_pages):
            p = jnp.exp(s - m)
            l = l + jnp.sum(p, axis=1, keepdims=True)
            o = o + jnp.sum(vbuf[slot, hd * per_head + j] * p, axis=1, keepdims=True)
        o_ref[0, r0:r0 + HEAD_DIM, :] = o / l


def _dec_moba_call(layer, page_table, sel, qrep, knrep, vnrep, kc, vc):
    nb = page_table.shape[0]
    rows, page = kc.shape[2], kc.shape[3]
    n_slab = MOBA_HEADS * MOBA_TOPK * (MOBA_BLOCK // page)
    rep_spec = pl.BlockSpec((1, rows, LANES), lambda i, pt, sl: (i, 0, 0))
    return pl.pallas_call(
        functools.partial(_dec_moba_kernel, layer=layer),
        grid_spec=pltpu.PrefetchScalarGridSpec(
            num_scalar_prefetch=2,
            grid=(nb,),
            in_specs=[rep_spec, rep_spec, rep_spec,
                      pl.BlockSpec(memory_space=pl.ANY), pl.BlockSpec(memory_space=pl.ANY)],
            out_specs=rep_spec,
            scratch_shapes=[pltpu.VMEM((2, n_slab, HEAD_DIM, page), F32),
                            pltpu.VMEM((2, n_slab, HEAD_DIM, page), F32),
                            pltpu.SemaphoreType.DMA((2, 2))],
        ),
        out_shape=jax.ShapeDtypeStruct((nb, rows, LANES), F32),
        compiler_params=_cparams(("arbitrary",)),
        name=f"dec_moba_l{layer}",
    )(page_table, sel, qrep, knrep, vnrep, kc, vc)


def _dec_diff_kernel(pt_ref, lam_ref, g_ref, qrep_ref, knrep_ref, vnew_ref, kc_ref, vc_ref, o_ref,
                     kbuf, vbuf, sem, m_s, l_s, acc_s, *, layer, n, lam_init):
    chunk = pl.program_id(1)
    slot = _stream_chunks(pt_ref, (kc_ref, vc_ref), (kbuf, vbuf), sem, layer, n)
    page = kbuf.shape[3]
    qrep = qrep_ref[0]

    @pl.when(chunk == 0)
    def _init():
        m_s[...] = jnp.full(m_s.shape, -jnp.inf, F32)
        l_s[...] = jnp.zeros_like(l_s)
        acc_s[...] = jnp.zeros_like(acc_s)

    for first in range(0, n, DEC_GROUP_PAGES):
        pages = range(first, min(first + DEC_GROUP_PAGES, n))
        s = jnp.concatenate([_head_scores(kbuf[slot, i], qrep) for i in pages], axis=1)
        m_prev = m_s[...]
        m_next = jnp.maximum(m_prev, jnp.max(s, axis=1, keepdims=True))
        alpha = jnp.exp(m_prev - m_next)
        p = jnp.exp(s - jnp.concatenate([m_next] * len(pages), axis=1))
        l_s[...] = alpha * l_s[...] + jnp.sum(p, axis=1, keepdims=True)
        m_s[...] = m_next
        pb = p.astype(BF16)
        for hd in range(DIFF_HEADS):
            v = jnp.concatenate(
                [vbuf[slot, i, pl.ds(hd, page, stride=DIFF_HEADS), :].astype(BF16) for i in pages], axis=0)
            acc_s[hd] = acc_s[hd] * alpha + jnp.dot(pb, v, preferred_element_type=F32)

    @pl.when(chunk == pl.num_programs(1) - 1)
    def _finish():
        n_map = 2 * DIFF_HEADS
        s_self = jnp.sum((qrep * knrep_ref[0]).reshape(n_map, HEAD_DIM, LANES), axis=1)
        m_prev = m_s[...]
        m_fin = jnp.maximum(m_prev, s_self)
        alpha = jnp.exp(m_prev - m_fin)
        p_self = jnp.exp(s_self - m_fin)
        l_fin = alpha * l_s[...] + p_self
        lam = _lambda(lam_ref, lam_init)
        for hd in range(DIFF_HEADS):
            maps = (acc_s[hd] * alpha + p_self * vnew_ref[0, hd:hd + 1, :]) / l_fin
            o = maps[2 * hd:2 * hd + 1, :] - lam * maps[2 * hd + 1:2 * hd + 2, :]
            o_ref[0, hd:hd + 1, :] = _rmsnorm(o, g_ref[...]) * (1.0 - lam_init)


def _dec_diff_call(layer, lam_init, page_table, lamv, g_sub, qrep, knrep, vnew, kc, vc, n):
    nb, n_pages = page_table.shape
    rows, page = kc.shape[2], kc.shape[3]
    n_map = 2 * DIFF_HEADS
    full = lambda a: pl.BlockSpec(a.shape, lambda i, c, pt: (0, 0))
    rep_spec = pl.BlockSpec((1, rows, LANES), lambda i, c, pt: (i, 0, 0))
    head_spec = pl.BlockSpec((1, DIFF_HEADS, LANES), lambda i, c, pt: (i, 0, 0))
    return pl.pallas_call(
        functools.partial(_dec_diff_kernel, layer=layer, n=n, lam_init=lam_init),
        grid_spec=pltpu.PrefetchScalarGridSpec(
            num_scalar_prefetch=1,
            grid=(nb, n_pages // n),
            in_specs=[full(lamv), full(g_sub), rep_spec, rep_spec, head_spec,
                      pl.BlockSpec(memory_space=pl.ANY), pl.BlockSpec(memory_space=pl.ANY)],
            out_specs=head_spec,
            scratch_shapes=[pltpu.VMEM((2, n, rows, page), F32),
                            pltpu.VMEM((2, n, vc.shape[2], vc.shape[3]), F32),
                            pltpu.SemaphoreType.DMA((2, 2)),
                            pltpu.VMEM((n_map, LANES), F32), pltpu.VMEM((n_map, LANES), F32),
                            pltpu.VMEM((DIFF_HEADS, n_map, LANES), F32)],
        ),
        out_shape=jax.ShapeDtypeStruct((nb, DIFF_HEADS, LANES), F32),
        compiler_params=_cparams(("arbitrary", "arbitrary")),
        name=f"dec_diff_l{layer}",
    )(page_table, lamv, g_sub, qrep, knrep, vnew, kc, vc)


def _rope_tables(pos):
    inv = ROPE_THETA ** (-jnp.arange(HALF_DIM, dtype=F32) / HALF_DIM)
    ang = pos.astype(F32)[:, None] * inv[None, :]
    cos, sin = jnp.cos(ang), jnp.sin(ang)
    reps = LANES // HEAD_DIM
    cos_r = jnp.tile(jnp.concatenate([cos, cos], axis=1), (1, reps))
    sin_r = jnp.tile(jnp.concatenate([-sin, sin], axis=1), (1, reps))
    return cos_r, sin_r, cos.T, sin.T


def _lane_rep(a):
    return jnp.broadcast_to(a[:, :, None], a.shape + (LANES,))


def _pages_per_chunk(n_pages, page):
    per_blk = MOBA_BLOCK // page
    for n in (16, 8, 4, 2):
        if n_pages % n == 0 and n % per_blk == 0:
            return n
    raise ValueError("page count must be a multiple of the pages per MoBA block")


def kernel(x_prompt, x_sample, cache_k_moba, cache_v_moba, cache_k_diff, cache_v_diff, page_table, c_prompt, c_sample, w_in, w_out, g_attn, g_mlp, w_ada, b_ada, w_up, w_down, lambda_q1, lambda_k1, lambda_q2, lambda_k2, g_subln, g_final):
    b, t, d = x_prompt.shape
    nb, dec_seq, _ = x_sample.shape
    depth = w_in.shape[0]
    n_pool, page = cache_k_moba.shape[1], cache_k_moba.shape[2]
    n_pages = page_table.shape[1]
    assert d == D_MODEL and dec_seq == 1 and t % (2 * MOBA_BLOCK) == 0
    assert MOBA_BLOCK % page == 0 and page == LANES and n_pages * page // MOBA_BLOCK >= MOBA_TOPK
    n_chunk_pages = _pages_per_chunk(n_pages, page)

    to_kt = lambda c: jnp.transpose(c, (0, 1, 3, 4, 2)).reshape(depth, n_pool, GROUP_W, page)
    kc_m, vc_m, kc_d = to_kt(cache_k_moba), to_kt(cache_v_moba), to_kt(cache_k_diff)
    vc_d = cache_v_diff.reshape(depth, n_pool, page * DIFF_HEADS, 2 * HEAD_DIM)

    mod = _ada_call(jnp.concatenate([c_prompt, c_sample], axis=0), w_ada, b_ada)
    _, _, cos_t, sin_t = _rope_tables(jnp.arange(t, dtype=jnp.int32))
    cos_s, sin_s, _, _ = _rope_tables(jnp.full((nb,), n_pages * page, jnp.int32))

    xp = x_prompt
    xs = x_sample.reshape(1, nb, d)
    kt_zero = jnp.zeros((depth, b, GROUP_W, t), F32)
    layered = (jnp.zeros((depth, b, DIFF_HEADS * t, LANES), F32), kt_zero, kt_zero, kt_zero)
    sample_new = []
    for l in range(depth):
        lam_init = 0.8 - 0.6 * math.exp(-0.3 * l)
        final = l == depth - 1
        lamv = jnp.stack([lambda_q1[l], lambda_k1[l], lambda_q2[l], lambda_k2[l]])
        g_sub = g_subln[l].reshape(1, -1)
        g_a, g_m = g_attn[l].reshape(1, d), g_mlp[l].reshape(1, d)
        g_f = g_final.reshape(1, d)
        mods_p = [m.reshape(b, 1, d) for m in jnp.split(mod[l, :b], 6, axis=-1)]
        mods_s = [m.reshape(1, nb, d) for m in jnp.split(mod[l, b:], 6, axis=-1)]
        win = w_in[l]
        groups = [win[:, k * GROUP_W:(k + 1) * GROUP_W] for k in range(6)]
        wa = groups[5].astype(BF16)
        wbt = jnp.concatenate(groups[:5], axis=1).T.astype(BF16)

        sh_a, sc_a, gt_a, sh_m, sc_m, gt_m = mods_p
        qm, qd, vd_all, ktm_all, vtm_all, ktd_all = _proj_call(
            l, depth, xp, g_a, sc_a, sh_a, wa, wbt, cos_t, sin_t, layered)
        layered = (vd_all, ktm_all, vtm_all, ktd_all)
        om = _moba_call(l, qm, ktm_all, vtm_all)
        od = _diff_call(l, lam_init, lamv, g_sub, qd, ktd_all, vd_all)
        x1, h2 = _outproj_call(f"outproj_l{l}", xp, om, od, w_out[l].astype(BF16), gt_a, g_m, sc_m, sh_m,
                               min(PROJ_TM, t), False)
        xp = _mlp_call(f"mlp_l{l}", h2, x1, w_up[l].astype(BF16), w_down[l].astype(BF16), gt_m, g_f,
                       min(MLP_TM, t), False, final)

        sh_a, sc_a, gt_a, sh_m, sc_m, gt_m = mods_s
        ps = _sproj_call(l, xs[0], g_a, sc_a[0], sh_a[0], win, cos_s, sin_s)
        qm_s, km_s, vm_s, qd_s, kd_s, vd_s = [ps[:, k * GROUP_W:(k + 1) * GROUP_W] for k in range(6)]
        sample_new.append((km_s, vm_s, kd_s, vd_s))
        qrep_m = _lane_rep(qm_s)
        sel = _dec_gate_call(l, page_table, qrep_m, kc_m, n_chunk_pages)[:, :, :, 0]
        om_s = _dec_moba_call(l, page_table, sel.reshape(nb, MOBA_TOPK * MOBA_HEADS), qrep_m, _lane_rep(km_s),
                              _lane_rep(vm_s), kc_m, vc_m)[:, :, 0]
        od_s = _dec_diff_call(l, lam_init, page_table, lamv, g_sub, _lane_rep(qd_s), _lane_rep(kd_s),
                              vd_s.reshape(nb, DIFF_HEADS, 2 * HEAD_DIM), kc_d, vc_d, n_chunk_pages)
        x1s, h2s = _outproj_call(f"outproj_s_l{l}", xs, om_s.reshape(1, nb, GROUP_W), od_s.reshape(1, nb, GROUP_W),
                                 w_out[l], gt_a, g_m, sc_m, sh_m, nb, True)
        xs = _mlp_call(f"mlp_s_l{l}", h2s, x1s, w_up[l], w_down[l], gt_m, g_f, nb, True, final)

    vd_all, ktm_all, vtm_all, ktd_all = layered
    from_kt = lambda a: jnp.transpose(a.reshape(depth, b, GROUP_W // HEAD_DIM, HEAD_DIM, t), (0, 1, 4, 2, 3))
    stack_s = lambda k, shape: jnp.stack([s[k] for s in sample_new]).reshape((depth, nb, 1) + shape)
    return (
        xp,
        xs.reshape(nb, 1, d),
        from_kt(ktm_all), from_kt(vtm_all), from_kt(ktd_all),
        vd_all.reshape(depth, b, t, DIFF_HEADS, 2 * HEAD_DIM),
        stack_s(0, (MOBA_HEADS, HEAD_DIM)), stack_s(1, (MOBA_HEADS, HEAD_DIM)),
        stack_s(2, (2 * DIFF_HEADS, HEAD_DIM)), stack_s(3, (DIFF_HEADS, 2 * HEAD_DIM)),
    )
```

```python
import functools
import math

import jax
import jax.numpy as jnp
from jax import lax
from jax.experimental import pallas as pl
from jax.experimental.pallas import tpu as pltpu

F32 = jnp.float32
BF16 = jnp.bfloat16
HIGHEST = lax.Precision.HIGHEST

D_MODEL = 1024
HEAD_DIM = 64
HALF_DIM = HEAD_DIM // 2
MOBA_HEADS = 8
DIFF_HEADS = 4
GROUP_W = 512
MOBA_BLOCK = 256
MOBA_TOPK = 3
ATTN_PAIR = 2 * MOBA_BLOCK
D_FF = 4 * D_MODEL
ROPE_THETA = 10000.0
EPS = 1e-6
NEG = -1e30
Q_SCALE = 1.0 / math.sqrt(HEAD_DIM)
Q_SCALE_LOG2 = Q_SCALE * math.log2(math.e)

LANES = 128
SUBLANES = 8
BF16_ROWS = 16
VMEM_LIMIT_BYTES = 52 * 1024 * 1024

PROJ_TM = 512
MLP_TM = 1024
MLP_TF = 1024
ADA_TN = 1536
DEC_GROUP_PAGES = 4


def _cparams(semantics):
    return pltpu.CompilerParams(dimension_semantics=semantics, vmem_limit_bytes=VMEM_LIMIT_BYTES)


def _rmsnorm(x, g):
    return x * lax.rsqrt(jnp.mean(x * x, axis=-1, keepdims=True) + EPS) * g


def _dot(a, b, exact):
    if exact:
        return jnp.dot(a.astype(F32), b.astype(F32), precision=HIGHEST, preferred_element_type=F32)
    return jnp.dot(a.astype(BF16), b.astype(BF16), preferred_element_type=F32)


def _dot_nt(a, b):
    return lax.dot_general(a, b, (((1,), (1,)), ((), ())), preferred_element_type=F32)


def _swap_halves(x):
    lane = lax.broadcasted_iota(jnp.int32, x.shape, 1)
    first_half = (lane % HEAD_DIM) < HALF_DIM
    return jnp.where(first_half, pltpu.roll(x, LANES - HALF_DIM, 1), pltpu.roll(x, HALF_DIM, 1))


def _rope_rows(p, cos, sin_signed):
    return p * cos + _swap_halves(p) * sin_signed


def _ada_kernel(c_ref, w_ref, b_ref, o_ref):
    c = c_ref[...]
    silu = c / (1.0 + jnp.exp(-c))
    o_ref[0] = _dot(silu, w_ref[0], True) + b_ref[0]


def _ada_call(c_all, w_ada, b_ada):
    depth, d, n = w_ada.shape
    rows = c_all.shape[0]
    return pl.pallas_call(
        _ada_kernel,
        grid=(depth, n // ADA_TN),
        in_specs=[
            pl.BlockSpec((rows, d), lambda l, j: (0, 0)),
            pl.BlockSpec((1, d, ADA_TN), lambda l, j: (l, 0, j)),
            pl.BlockSpec((1, 1, ADA_TN), lambda l, j: (l, 0, j)),
        ],
        out_specs=pl.BlockSpec((1, rows, ADA_TN), lambda l, j: (l, 0, j)),
        out_shape=jax.ShapeDtypeStruct((depth, rows, n), F32),
        compiler_params=_cparams(("parallel", "parallel")),
        name="adaln",
    )(c_all, w_ada, b_ada.reshape(depth, 1, n))


def _proj_kernel(x_ref, g_ref, sc_ref, sh_ref, wa_ref, wbt_ref, ct_ref, st_ref, *rest):
    qmt_ref, qdt_ref, vd_ref, ktm_ref, vtm_ref, ktd_ref = rest[-6:]
    tm = x_ref.shape[1]
    h = (_rmsnorm(x_ref[0], g_ref[...]) * (1.0 + sc_ref[0]) + sh_ref[0]).astype(BF16)

    pa = jnp.dot(h, wa_ref[...], preferred_element_type=F32)
    for hd in range(DIFF_HEADS):
        vd_ref[0, 0, pl.ds(hd, tm, stride=DIFF_HEADS), :] = pa[:, hd * LANES:(hd + 1) * LANES]

    pb = _dot_nt(wbt_ref[...], h)
    cos_t = ct_ref[...]
    sin_t = st_ref[...]
    vtm_ref[0, 0] = pb[2 * GROUP_W:3 * GROUP_W]
    for src, dst, scale in ((0, qmt_ref, Q_SCALE_LOG2), (GROUP_W, ktm_ref, None),
                            (3 * GROUP_W, qdt_ref, Q_SCALE_LOG2), (4 * GROUP_W, ktd_ref, None)):
        for hd in range(GROUP_W // HEAD_DIM):
            r0 = src + hd * HEAD_DIM
            x1 = pb[r0:r0 + HALF_DIM]
            x2 = pb[r0 + HALF_DIM:r0 + HEAD_DIM]
            y1 = x1 * cos_t - x2 * sin_t
            y2 = x1 * sin_t + x2 * cos_t
            o0 = hd * HEAD_DIM
            if scale is None:
                dst[0, 0, o0:o0 + HALF_DIM, :] = y1
                dst[0, 0, o0 + HALF_DIM:o0 + HEAD_DIM, :] = y2
            else:
                dst[0, o0:o0 + HALF_DIM, :] = (y1 * scale).astype(dst.dtype)
                dst[0, o0 + HALF_DIM:o0 + HEAD_DIM, :] = (y2 * scale).astype(dst.dtype)


def _proj_call(layer, depth, x, g, sc, sh, wa, wbt, cos_t, sin_t, layered):
    b, t, d = x.shape
    tm = min(PROJ_TM, t)
    in_specs = [
        pl.BlockSpec((1, tm, d), lambda i, j: (i, j, 0)),
        pl.BlockSpec((1, d), lambda i, j: (0, 0)),
        pl.BlockSpec((1, 1, d), lambda i, j: (i, 0, 0)),
        pl.BlockSpec((1, 1, d), lambda i, j: (i, 0, 0)),
        pl.BlockSpec(wa.shape, lambda i, j: (0, 0)),
        pl.BlockSpec(wbt.shape, lambda i, j: (0, 0)),
        pl.BlockSpec((HALF_DIM, tm), lambda i, j: (0, j)),
        pl.BlockSpec((HALF_DIM, tm), lambda i, j: (0, j)),
    ]
    args = [x, g, sc, sh, wa, wbt, cos_t, sin_t]
    aliases = {}
    for k, arr in enumerate(layered):
        in_specs.append(pl.BlockSpec(memory_space=pl.ANY))
        aliases[len(args)] = 2 + k
        args.append(arr)
    kt_shape = jax.ShapeDtypeStruct((depth, b, GROUP_W, t), F32)
    kt_spec = pl.BlockSpec((1, 1, GROUP_W, tm), lambda i, j: (layer, i, 0, j))
    qt_spec = pl.BlockSpec((1, GROUP_W, tm), lambda i, j: (i, 0, j))
    return pl.pallas_call(
        _proj_kernel,
        grid=(b, t // tm),
        in_specs=in_specs,
        out_specs=[
            qt_spec, qt_spec,
            pl.BlockSpec((1, 1, DIFF_HEADS * tm, LANES), lambda i, j: (layer, i, j, 0)),
            kt_spec, kt_spec, kt_spec,
        ],
        out_shape=[
            jax.ShapeDtypeStruct((b, GROUP_W, t), BF16),
            jax.ShapeDtypeStruct((b, GROUP_W, t), BF16),
            jax.ShapeDtypeStruct((depth, b, DIFF_HEADS * t, LANES), F32),
            kt_shape, kt_shape, kt_shape,
        ],
        input_output_aliases=aliases,
        compiler_params=_cparams(("parallel", "parallel")),
        name=f"proj_l{layer}",
    )(*args)


def _tile_rows(row, n_rows):
    return jnp.concatenate([row] * (n_rows // SUBLANES), axis=0)


def _stash_scores(scores, slot, s_s, mc_s):
    for idx, s in enumerate(scores):
        s_s[slot, idx] = s
        mc_s[slot, idx] = jnp.broadcast_to(jnp.max(s, axis=0, keepdims=True), mc_s.shape[2:])


def _fold_scores(slot, values, s_s, mc_s, m_s, l_s, acc_s):
    for idx, vt in enumerate(values):
        s = s_s[slot, idx]
        m_prev = m_s[idx]
        m_next = jnp.maximum(m_prev, mc_s[slot, idx])
        alpha = jnp.exp2(m_prev - m_next)
        p = jnp.exp2(s - _tile_rows(m_next, s.shape[0]))
        l_s[idx] = alpha * l_s[idx] + jnp.broadcast_to(jnp.sum(p, axis=0, keepdims=True), alpha.shape)
        acc_s[idx] = (acc_s[idx] * _tile_rows(alpha, vt.shape[0])
                      + jnp.dot(vt, p.astype(BF16), preferred_element_type=F32))
        m_s[idx] = m_next


def _attend(last, scores_fn, values_fn, s_s, mc_s, m_s, l_s, acc_s):
    m_s[...] = jnp.full(m_s.shape, -jnp.inf, F32)
    l_s[...] = jnp.zeros_like(l_s)
    acc_s[...] = jnp.zeros_like(acc_s)
    _stash_scores(scores_fn(last, True), 0, s_s, mc_s)

    def body(it, carry):
        slot = lax.rem(it, 2)
        ahead = scores_fn(it, False)
        _fold_scores(slot, values_fn(jnp.where(it == 0, last, it - 1)), s_s, mc_s, m_s, l_s, acc_s)
        _stash_scores(ahead, 1 - slot, s_s, mc_s)
        return carry

    lax.fori_loop(0, last, body, 0)
    _fold_scores(lax.rem(last, 2), values_fn(jnp.maximum(last - 1, 0)), s_s, mc_s, m_s, l_s, acc_s)


def _split_heads(qt):
    zero = jnp.zeros((HEAD_DIM, qt.shape[1]), qt.dtype)
    return (jnp.concatenate([qt[:HEAD_DIM], zero], axis=0), jnp.concatenate([zero, qt[HEAD_DIM:]], axis=0))


def _causal_pair(pair, qi, n_keys, tq):
    key = pair * n_keys + lax.broadcasted_iota(jnp.int32, (n_keys, tq), 0)
    qry = qi * tq + lax.broadcasted_iota(jnp.int32, (n_keys, tq), 1)
    return key <= qry


def _top_blocks_t(gate, q_blk, n_blk):
    blk = lax.broadcasted_iota(jnp.int32, gate.shape, 0)
    past = blk < q_blk
    g = jnp.where(past, gate, -jnp.inf)
    rank = jnp.zeros(gate.shape, F32)
    for m in range(n_blk):
        gm = jnp.broadcast_to(g[m:m + 1, :], gate.shape)
        beats = (gm > g) | ((gm == g) & (m < blk))
        rank = rank + jnp.where(beats, 1.0, 0.0)
    keep = (past & (rank < float(MOBA_TOPK))) | (blk == q_blk)
    return jnp.where(keep, 1.0, 0.0)


def _moba_kernel(qt_ref, kt_ref, vt_ref, o_ref, k_s, vt_s, kmh_s, kml_s, sel_s, s_s, mc_s, m_s, l_s, acc_s):
    qi = pl.program_id(2)
    n_pair, n_keys, _ = k_s.shape
    blk = n_keys // 2
    n_blk = 2 * n_pair
    tq = qt_ref.shape[2]

    @pl.when(qi == 0)
    def _stage():
        row = lax.broadcasted_iota(jnp.int32, kmh_s.shape, 0)
        km = jnp.zeros(kmh_s.shape, F32)
        for n in range(n_blk):
            k = kt_ref[0, 0, :, n * blk:(n + 1) * blk].T
            k_s[n // 2, (n % 2) * blk:(n % 2 + 1) * blk, :] = k.astype(BF16)
            vt_s[n // 2, :, (n % 2) * blk:(n % 2 + 1) * blk] = vt_ref[0, 0, :, n * blk:(n + 1) * blk].astype(BF16)
            km = jnp.where(row == n, jnp.sum(k, axis=0, keepdims=True) * (1.0 / blk), km)
        hi = km.astype(BF16)
        kmh_s[...] = hi
        kml_s[...] = (km - hi.astype(F32)).astype(BF16)

    q_heads = _split_heads(qt_ref[0])
    lane = lax.broadcasted_iota(jnp.int32, sel_s.shape[1:], 1)
    q_blk = qi * (tq // blk)
    for k in range(1, tq // blk):
        q_blk = q_blk + jnp.where(lane >= k * blk, 1, 0)
    for idx, qh in enumerate(q_heads):
        gate = (jnp.dot(kmh_s[...], qh, preferred_element_type=F32)
                + jnp.dot(kml_s[...], qh, preferred_element_type=F32))
        sel_s[idx] = _top_blocks_t(gate, q_blk, n_blk)

    def scores(pair, with_diagonal):
        out = []
        for idx, qh in enumerate(q_heads):
            s = jnp.dot(k_s[pair], qh, preferred_element_type=F32)
            keep = jnp.concatenate([
                jnp.broadcast_to(sel_s[idx, pl.ds(2 * pair, 1), :], (blk, tq)),
                jnp.broadcast_to(sel_s[idx, pl.ds(2 * pair + 1, 1), :], (blk, tq))], axis=0) > 0.0
            s = jnp.where(keep, s, NEG)
            if with_diagonal:
                s = jnp.where(_causal_pair(pair, qi, n_keys, tq), s, NEG)
            out.append(s)
        return out

    def values(pair):
        return [vt_s[pair, idx * HEAD_DIM:(idx + 1) * HEAD_DIM, :] for idx in range(2)]

    _attend(qi, scores, values, s_s, mc_s, m_s, l_s, acc_s)
    ot = jnp.concatenate([acc_s[idx] / _tile_rows(l_s[idx], HEAD_DIM) for idx in range(2)], axis=0)
    o_ref[0] = ot.T.astype(o_ref.dtype)


def _attn_scratch(n_pair, n_keys, tq, v_rows):
    return [
        pltpu.VMEM((n_pair, n_keys, LANES), BF16),
        pltpu.VMEM((n_pair, LANES, n_keys), BF16),
        pltpu.VMEM((2, 2, n_keys, tq), F32),
        pltpu.VMEM((2, 2, SUBLANES, tq), F32),
        pltpu.VMEM((2, SUBLANES, tq), F32),
        pltpu.VMEM((2, SUBLANES, tq), F32),
        pltpu.VMEM((2, v_rows, tq), F32),
    ]


def _moba_call(layer, qt, kt_all, vt_all):
    b, _, t = qt.shape
    n_blk = t // MOBA_BLOCK
    tq = ATTN_PAIR
    kv_spec = pl.BlockSpec((1, 1, LANES, t), lambda i, h, j: (layer, i, h, 0))
    k_s, vt_s, s_s, mc_s, m_s, l_s, acc_s = _attn_scratch(t // ATTN_PAIR, ATTN_PAIR, tq, HEAD_DIM)
    mean_rows = -(-n_blk // BF16_ROWS) * BF16_ROWS
    return pl.pallas_call(
        _moba_kernel,
        grid=(b, GROUP_W // LANES, t // tq),
        in_specs=[pl.BlockSpec((1, LANES, tq), lambda i, h, j: (i, h, j)), kv_spec, kv_spec],
        out_specs=pl.BlockSpec((1, tq, LANES), lambda i, h, j: (i, j, h)),
        out_shape=jax.ShapeDtypeStruct((b, t, GROUP_W), BF16),
        scratch_shapes=[
            k_s, vt_s,
            pltpu.VMEM((mean_rows, LANES), BF16),
            pltpu.VMEM((mean_rows, LANES), BF16),
            pltpu.VMEM((2, mean_rows, tq), F32),
            s_s, mc_s, m_s, l_s, acc_s,
        ],
        compiler_params=_cparams(("parallel", "parallel", "arbitrary")),
        name=f"moba_l{layer}",
    )(qt, kt_all, vt_all)


def _lambda(lam_ref, lam_init):
    a = jnp.sum(lam_ref[0:1, :] * lam_ref[1:2, :], axis=1, keepdims=True)
    b = jnp.sum(lam_ref[2:3, :] * lam_ref[3:4, :], axis=1, keepdims=True)
    return jnp.exp(a) - jnp.exp(b) + lam_init


def _diff_kernel(lam_ref, g_ref, qt_ref, kt_ref, v_ref, o_ref, k_s, vt_s, s_s, mc_s, m_s, l_s, acc_s, *, lam_init):
    hd = pl.program_id(1)
    qi = pl.program_id(2)
    n_pair, n_keys, _ = k_s.shape
    blk = n_keys // 2
    tq = qt_ref.shape[2]

    @pl.when(qi == 0)
    def _stage():
        for n in range(2 * n_pair):
            k = kt_ref[0, 0, :, n * blk:(n + 1) * blk].T
            k_s[n // 2, (n % 2) * blk:(n % 2 + 1) * blk, :] = k.astype(BF16)
            v = v_ref[0, 0, pl.ds(n * blk * DIFF_HEADS + hd, blk, stride=DIFF_HEADS), :]
            vt_s[n // 2, :, (n % 2) * blk:(n % 2 + 1) * blk] = v.T.astype(BF16)

    q_maps = _split_heads(qt_ref[0])

    def scores(pair, with_diagonal):
        out = [jnp.dot(k_s[pair], qm, preferred_element_type=F32) for qm in q_maps]
        if with_diagonal:
            causal = _causal_pair(pair, qi, n_keys, tq)
            out = [jnp.where(causal, s, NEG) for s in out]
        return out

    def values(pair):
        vt = vt_s[pair]
        return [vt, vt]

    _attend(qi, scores, values, s_s, mc_s, m_s, l_s, acc_s)
    o1, o2 = [acc_s[idx] / _tile_rows(l_s[idx], LANES) for idx in range(2)]
    ot = o1 - _lambda(lam_ref, lam_init) * o2
    ot = ot * lax.rsqrt(jnp.mean(ot * ot, axis=0, keepdims=True) + EPS)
    o_ref[0] = (ot.T * g_ref[...] * (1.0 - lam_init)).astype(o_ref.dtype)


def _diff_call(layer, lam_init, lamv, g_sub, qt, kt_all, v_all):
    b, _, t = qt.shape
    tq = ATTN_PAIR
    return pl.pallas_call(
        functools.partial(_diff_kernel, lam_init=lam_init),
        grid=(b, DIFF_HEADS, t // tq),
        in_specs=[
            pl.BlockSpec(lamv.shape, lambda i, h, j: (0, 0)),
            pl.BlockSpec(g_sub.shape, lambda i, h, j: (0, 0)),
            pl.BlockSpec((1, LANES, tq), lambda i, h, j: (i, h, j)),
            pl.BlockSpec((1, 1, LANES, t), lambda i, h, j: (layer, i, h, 0)),
            pl.BlockSpec((1, 1, DIFF_HEADS * t, LANES), lambda i, h, j: (layer, i, 0, 0)),
        ],
        out_specs=pl.BlockSpec((1, tq, LANES), lambda i, h, j: (i, j, h)),
        out_shape=jax.ShapeDtypeStruct((b, t, GROUP_W), BF16),
        scratch_shapes=_attn_scratch(t // ATTN_PAIR, ATTN_PAIR, tq, 2 * HEAD_DIM),
        compiler_params=_cparams(("parallel", "parallel", "arbitrary")),
        name=f"diff_l{layer}",
    )(lamv, g_sub, qt, kt_all, v_all)


def _outproj_kernel(x_ref, om_ref, od_ref, w_ref, gt_ref, g_ref, sc_ref, sh_ref, x1_ref, h2_ref, *, exact):
    o = _dot(om_ref[0], w_ref[0:GROUP_W, :], exact) + _dot(od_ref[0], w_ref[GROUP_W:2 * GROUP_W, :], exact)
    x1 = x_ref[0] + gt_ref[0] * o
    x1_ref[0] = x1
    h2_ref[0] = (_rmsnorm(x1, g_ref[...]) * (1.0 + sc_ref[0]) + sh_ref[0]).astype(h2_ref.dtype)


def _mod_spec(mod, tm):
    d = mod.shape[-1]
    if mod.shape[1] == 1:
        return pl.BlockSpec((1, 1, d), lambda i, j, *_: (i, 0, 0))
    return pl.BlockSpec((1, tm, d), lambda i, j, *_: (i, j, 0))


def _outproj_call(name, x, om, od, w, gt, g, sc, sh, tm, exact):
    grp, rows, d = x.shape
    row_spec = lambda width: pl.BlockSpec((1, tm, width), lambda i, j: (i, j, 0))
    return pl.pallas_call(
        functools.partial(_outproj_kernel, exact=exact),
        grid=(grp, rows // tm),
        in_specs=[
            row_spec(d), row_spec(GROUP_W), row_spec(GROUP_W),
            pl.BlockSpec(w.shape, lambda i, j: (0, 0)),
            _mod_spec(gt, tm),
            pl.BlockSpec((1, d), lambda i, j: (0, 0)),
            _mod_spec(sc, tm), _mod_spec(sh, tm),
        ],
        out_specs=[row_spec(d), row_spec(d)],
        out_shape=[jax.ShapeDtypeStruct(x.shape, F32), jax.ShapeDtypeStruct(x.shape, F32 if exact else BF16)],
        compiler_params=_cparams(("parallel", "parallel")),
        name=name,
    )(x, om, od, w, gt, g, sc, sh)


def _mlp_kernel(h_ref, x1_ref, wu_ref, wd_ref, gt_ref, gf_ref, o_ref, acc_s, *, exact, final_norm):
    f = pl.program_id(2)

    @pl.when(f == 0)
    def _zero():
        acc_s[...] = jnp.zeros_like(acc_s)

    u = jnp.maximum(_dot(h_ref[0], wu_ref[...], exact), 0.0)
    acc_s[...] += _dot(u * u, wd_ref[...], exact)

    @pl.when(f == pl.num_programs(2) - 1)
    def _finish():
        x2 = x1_ref[0] + gt_ref[0] * acc_s[...]
        o_ref[0] = _rmsnorm(x2, gf_ref[...]) if final_norm else x2


def _mlp_call(name, h, x1, wu, wd, gt, gf, tm, exact, final_norm):
    grp, rows, d = x1.shape
    ff = wu.shape[1]
    tf = min(MLP_TF, ff)
    row_spec = pl.BlockSpec((1, tm, d), lambda i, j, f: (i, j, 0))
    return pl.pallas_call(
        functools.partial(_mlp_kernel, exact=exact, final_norm=final_norm),
        grid=(grp, rows // tm, ff // tf),
        in_specs=[
            row_spec, row_spec,
            pl.BlockSpec((d, tf), lambda i, j, f: (0, f)),
            pl.BlockSpec((tf, d), lambda i, j, f: (f, 0)),
            _mod_spec(gt, tm),
            pl.BlockSpec((1, d), lambda i, j, f: (0, 0)),
        ],
        out_specs=row_spec,
        out_shape=jax.ShapeDtypeStruct(x1.shape, F32),
        scratch_shapes=[pltpu.VMEM((tm, d), F32)],
        compiler_params=_cparams(("parallel", "parallel", "arbitrary")),
        name=name,
    )(h, x1, wu, wd, gt, gf)


def _sproj_kernel(x_ref, g_ref, sc_ref, sh_ref, w_ref, cr_ref, sr_ref, o_ref):
    j = pl.program_id(0)
    h = _rmsnorm(x_ref[...], g_ref[...]) * (1.0 + sc_ref[...]) + sh_ref[...]
    p = _dot(h, w_ref[...], True)
    is_q = (j == 0) | (j == 3)
    is_rope = is_q | (j == 1) | (j == 4)
    scale = jnp.where(is_q, Q_SCALE, 1.0)
    cos_r = cr_ref[...]
    sin_r = sr_ref[...]
    for c in range(GROUP_W // LANES):
        pc = p[:, c * LANES:(c + 1) * LANES]
        o_ref[:, c * LANES:(c + 1) * LANES] = jnp.where(is_rope, _rope_rows(pc, cos_r, sin_r), pc) * scale


def _sproj_call(layer, x, g, sc, sh, w, cos_r, sin_r):
    rows, d = x.shape
    n = w.shape[1]
    full = lambda a: pl.BlockSpec(a.shape, lambda j: (0, 0))
    return pl.pallas_call(
        _sproj_kernel,
        grid=(n // GROUP_W,),
        in_specs=[full(x), full(g), full(sc), full(sh), pl.BlockSpec((d, GROUP_W), lambda j: (0, j)),
                  full(cos_r), full(sin_r)],
        out_specs=pl.BlockSpec((rows, GROUP_W), lambda j: (0, j)),
        out_shape=jax.ShapeDtypeStruct((rows, n), F32),
        compiler_params=_cparams(("parallel",)),
        name=f"sproj_l{layer}",
    )(x, g, sc, sh, w, cos_r, sin_r)


def _head_scores(kt, qrep):
    n_heads = kt.shape[0] // HEAD_DIM
    prod = (kt * qrep).reshape(n_heads, HEAD_DIM // SUBLANES, SUBLANES, LANES)
    return jnp.sum(jnp.sum(prod, axis=1), axis=1)


def _chunk_copies(pt_ref, srcs, bufs, sem, layer, seq, chunk, slot, n):
    out = []
    for k, (src, buf) in enumerate(zip(srcs, bufs)):
        for i in range(n):
            page = pt_ref[seq, chunk * n + i]
            out.append(pltpu.make_async_copy(src.at[layer, page], buf.at[slot, i], sem.at[k, slot]))
    return out


def _stream_chunks(pt_ref, srcs, bufs, sem, layer, n):
    seq = pl.program_id(0)
    chunk = pl.program_id(1)
    n_chunk = pl.num_programs(1)
    step = seq * n_chunk + chunk
    slot = lax.rem(step, 2)

    @pl.when(step == 0)
    def _prime():
        for cp in _chunk_copies(pt_ref, srcs, bufs, sem, layer, 0, 0, 0, n):
            cp.start()

    @pl.when(step + 1 < pl.num_programs(0) * n_chunk)
    def _prefetch():
        wrap = chunk + 1 == n_chunk
        nseq = jnp.where(wrap, seq + 1, seq)
        nchunk = jnp.where(wrap, 0, chunk + 1)
        for cp in _chunk_copies(pt_ref, srcs, bufs, sem, layer, nseq, nchunk, 1 - slot, n):
            cp.start()

    for cp in _chunk_copies(pt_ref, srcs, bufs, sem, layer, seq, chunk, slot, n):
        cp.wait()
    return slot


def _dec_gate_kernel(pt_ref, qrep_ref, kc_ref, sel_ref, buf, sem, g_s, *, layer, n):
    chunk = pl.program_id(1)
    slot = _stream_chunks(pt_ref, (kc_ref,), (buf,), sem, layer, n)
    qrep = qrep_ref[0]
    per_blk = MOBA_BLOCK // buf.shape[3]
    for k in range(n // per_blk):
        pages = buf[slot, per_blk * k]
        for extra in range(1, per_blk):
            pages = pages + buf[slot, per_blk * k + extra]
        g_s[chunk * (n // per_blk) + k] = _head_scores(pages, qrep)

    @pl.when(chunk == pl.num_programs(1) - 1)
    def _select():
        n_blk = g_s.shape[0]
        flat = g_s[...].reshape(n_blk * MOBA_HEADS, LANES)
        tot = jnp.broadcast_to(jnp.sum(flat, axis=1, keepdims=True), flat.shape).reshape(g_s.shape)
        blk = lax.broadcasted_iota(jnp.int32, g_s.shape, 0).astype(F32)
        for r in range(MOBA_TOPK):
            m = jnp.max(tot, axis=0, keepdims=True)
            first = jnp.min(jnp.where(tot == m, blk, float(n_blk)), axis=0, keepdims=True)
            sel_ref[0, r] = first[0].astype(jnp.int32)
            tot = jnp.where(blk == first, -jnp.inf, tot)


def _dec_gate_call(layer, page_table, qrep, kc, n):
    nb, n_pages = page_table.shape
    rows, page = kc.shape[2], kc.shape[3]
    n_blk = n_pages * page // MOBA_BLOCK
    return pl.pallas_call(
        functools.partial(_dec_gate_kernel, layer=layer, n=n),
        grid_spec=pltpu.PrefetchScalarGridSpec(
            num_scalar_prefetch=1,
            grid=(nb, n_pages // n),
            in_specs=[pl.BlockSpec((1, rows, LANES), lambda i, c, pt: (i, 0, 0)),
                      pl.BlockSpec(memory_space=pl.ANY)],
            out_specs=pl.BlockSpec((1, MOBA_TOPK, MOBA_HEADS, LANES), lambda i, c, pt: (i, 0, 0, 0)),
            scratch_shapes=[pltpu.VMEM((2, n, rows, page), F32), pltpu.SemaphoreType.DMA((1, 2)),
                            pltpu.VMEM((n_blk, MOBA_HEADS, LANES), F32)],
        ),
        out_shape=jax.ShapeDtypeStruct((nb, MOBA_TOPK, MOBA_HEADS, LANES), jnp.int32),
        compiler_params=_cparams(("arbitrary", "arbitrary")),
        name=f"dec_gate_l{layer}",
    )(page_table, qrep, kc)


def _dec_moba_kernel(pt_ref, sel_ref, qrep_ref, knrep_ref, vnrep_ref, kc_ref, vc_ref, o_ref, kbuf, vbuf, sem,
                     *, layer):
    seq = pl.program_id(0)
    slot = lax.rem(seq, 2)
    per_blk = MOBA_BLOCK // kbuf.shape[3]
    per_head = MOBA_TOPK * per_blk

    def copies(s, sl):
        out = []
        for hd in range(MOBA_HEADS):
            rows = pl.ds(hd * HEAD_DIM, HEAD_DIM)
            for r in range(MOBA_TOPK):
                blk = sel_ref[s, r * MOBA_HEADS + hd]
                for pg in range(per_blk):
                    page = pt_ref[s, per_blk * blk + pg]
                    j = hd * per_head + r * per_blk + pg
                    out.append(pltpu.make_async_copy(kc_ref.at[layer, page, rows], kbuf.at[sl, j], sem.at[0, sl]))
                    out.append(pltpu.make_async_copy(vc_ref.at[layer, page, rows], vbuf.at[sl, j], sem.at[1, sl]))
        return out

    @pl.when(seq == 0)
    def _prime():
        for cp in copies(0, 0):
            cp.start()

    @pl.when(seq + 1 < pl.num_programs(0))
    def _prefetch():
        for cp in copies(seq + 1, 1 - slot):
            cp.start()

    for cp in copies(seq, slot):
        cp.wait()

    for hd in range(MOBA_HEADS):
        r0 = hd * HEAD_DIM
        q = qrep_ref[0, r0:r0 + HEAD_DIM, :]
        s_self = jnp.sum(q * knrep_ref[0, r0:r0 + HEAD_DIM, :], axis=0, keepdims=True)
        s_pages = [jnp.sum(kbuf[slot, hd * per_head + j] * q, axis=0, keepdims=True) for j in range(per_head)]
        m = s_self
        for s in s_pages:
            m = jnp.maximum(m, jnp.max(s, axis=1, keepdims=True))
        p_self = jnp.exp(s_self - m)
        l = p_self
        o = vnrep_ref[0, r0:r0 + HEAD_DIM, :] * p_self
        for j, s in enumerate(s_pages):
            p = jnp.exp(s - m)
            l = l + jnp.sum(p, axis=1, keepdims=True)
            o = o + jnp.sum(vbuf[slot, hd * per_head + j] * p, axis=1, keepdims=True)
        o_ref[0, r0:r0 + HEAD_DIM, :] = o / l


def _dec_moba_call(layer, page_table, sel, qrep, knrep, vnrep, kc, vc):
    nb = page_table.shape[0]
    rows, page = kc.shape[2], kc.shape[3]
    n_slab = MOBA_HEADS * MOBA_TOPK * (MOBA_BLOCK // page)
    rep_spec = pl.BlockSpec((1, rows, LANES), lambda i, pt, sl: (i, 0, 0))
    return pl.pallas_call(
        functools.partial(_dec_moba_kernel, layer=layer),
        grid_spec=pltpu.PrefetchScalarGridSpec(
            num_scalar_prefetch=2,
            grid=(nb,),
            in_specs=[rep_spec, rep_spec, rep_spec,
                      pl.BlockSpec(memory_space=pl.ANY), pl.BlockSpec(memory_space=pl.ANY)],
            out_specs=rep_spec,
            scratch_shapes=[pltpu.VMEM((2, n_slab, HEAD_DIM, page), F32),
                            pltpu.VMEM((2, n_slab, HEAD_DIM, page), F32),
                            pltpu.SemaphoreType.DMA((2, 2))],
        ),
        out_shape=jax.ShapeDtypeStruct((nb, rows, LANES), F32),
        compiler_params=_cparams(("arbitrary",)),
        name=f"dec_moba_l{layer}",
    )(page_table, sel, qrep, knrep, vnrep, kc, vc)


def _dec_diff_kernel(pt_ref, lam_ref, g_ref, qrep_ref, knrep_ref, vnew_ref, kc_ref, vc_ref, o_ref,
                     kbuf, vbuf, sem, m_s, l_s, acc_s, *, layer, n, lam_init):
    chunk = pl.program_id(1)
    slot = _stream_chunks(pt_ref, (kc_ref, vc_ref), (kbuf, vbuf), sem, layer, n)
    page = kbuf.shape[3]
    qrep = qrep_ref[0]

    @pl.when(chunk == 0)
    def _init():
        m_s[...] = jnp.full(m_s.shape, -jnp.inf, F32)
        l_s[...] = jnp.zeros_like(l_s)
        acc_s[...] = jnp.zeros_like(acc_s)

    for first in range(0, n, DEC_GROUP_PAGES):
        pages = range(first, min(first + DEC_GROUP_PAGES, n))
        s = jnp.concatenate([_head_scores(kbuf[slot, i], qrep) for i in pages], axis=1)
        m_prev = m_s[...]
        m_next = jnp.maximum(m_prev, jnp.max(s, axis=1, keepdims=True))
        alpha = jnp.exp(m_prev - m_next)
        p = jnp.exp(s - jnp.concatenate([m_next] * len(pages), axis=1))
        l_s[...] = alpha * l_s[...] + jnp.sum(p, axis=1, keepdims=True)
        m_s[...] = m_next
        pb = p.astype(BF16)
        for hd in range(DIFF_HEADS):
            v = jnp.concatenate(
                [vbuf[slot, i, pl.ds(hd, page, stride=DIFF_HEADS), :].astype(BF16) for i in pages], axis=0)
            acc_s[hd] = acc_s[hd] * alpha + jnp.dot(pb, v, preferred_element_type=F32)

    @pl.when(chunk == pl.num_programs(1) - 1)
    def _finish():
        n_map = 2 * DIFF_HEADS
        s_self = jnp.sum((qrep * knrep_ref[0]).reshape(n_map, HEAD_DIM, LANES), axis=1)
        m_prev = m_s[...]
        m_fin = jnp.maximum(m_prev, s_self)
        alpha = jnp.exp(m_prev - m_fin)
        p_self = jnp.exp(s_self - m_fin)
        l_fin = alpha * l_s[...] + p_self
        lam = _lambda(lam_ref, lam_init)
        for hd in range(DIFF_HEADS):
            maps = (acc_s[hd] * alpha + p_self * vnew_ref[0, hd:hd + 1, :]) / l_fin
            o = maps[2 * hd:2 * hd + 1, :] - lam * maps[2 * hd + 1:2 * hd + 2, :]
            o_ref[0, hd:hd + 1, :] = _rmsnorm(o, g_ref[...]) * (1.0 - lam_init)


def _dec_diff_call(layer, lam_init, page_table, lamv, g_sub, qrep, knrep, vnew, kc, vc, n):
    nb, n_pages = page_table.shape
    rows, page = kc.shape[2], kc.shape[3]
    n_map = 2 * DIFF_HEADS
    full = lambda a: pl.BlockSpec(a.shape, lambda i, c, pt: (0, 0))
    rep_spec = pl.BlockSpec((1, rows, LANES), lambda i, c, pt: (i, 0, 0))
    head_spec = pl.BlockSpec((1, DIFF_HEADS, LANES), lambda i, c, pt: (i, 0, 0))
    return pl.pallas_call(
        functools.partial(_dec_diff_kernel, layer=layer, n=n, lam_init=lam_init),
        grid_spec=pltpu.PrefetchScalarGridSpec(
            num_scalar_prefetch=1,
            grid=(nb, n_pages // n),
            in_specs=[full(lamv), full(g_sub), rep_spec, rep_spec, head_spec,
                      pl.BlockSpec(memory_space=pl.ANY), pl.BlockSpec(memory_space=pl.ANY)],
            out_specs=head_spec,
            scratch_shapes=[pltpu.VMEM((2, n, rows, page), F32),
                            pltpu.VMEM((2, n, vc.shape[2], vc.shape[3]), F32),
                            pltpu.SemaphoreType.DMA((2, 2)),
                            pltpu.VMEM((n_map, LANES), F32), pltpu.VMEM((n_map, LANES), F32),
                            pltpu.VMEM((DIFF_HEADS, n_map, LANES), F32)],
        ),
        out_shape=jax.ShapeDtypeStruct((nb, DIFF_HEADS, LANES), F32),
        compiler_params=_cparams(("arbitrary", "arbitrary")),
        name=f"dec_diff_l{layer}",
    )(page_table, lamv, g_sub, qrep, knrep, vnew, kc, vc)


def _rope_tables(pos):
    inv = ROPE_THETA ** (-jnp.arange(HALF_DIM, dtype=F32) / HALF_DIM)
    ang = pos.astype(F32)[:, None] * inv[None, :]
    cos, sin = jnp.cos(ang), jnp.sin(ang)
    reps = LANES // HEAD_DIM
    cos_r = jnp.tile(jnp.concatenate([cos, cos], axis=1), (1, reps))
    sin_r = jnp.tile(jnp.concatenate([-sin, sin], axis=1), (1, reps))
    return cos_r, sin_r, cos.T, sin.T


def _lane_rep(a):
    return jnp.broadcast_to(a[:, :, None], a.shape + (LANES,))


def _pages_per_chunk(n_pages, page):
    per_blk = MOBA_BLOCK // page
    for n in (16, 8, 4, 2):
        if n_pages % n == 0 and n % per_blk == 0:
            return n
    raise ValueError("page count must be a multiple of the pages per MoBA block")


def kernel(x_prompt, x_sample, cache_k_moba, cache_v_moba, cache_k_diff, cache_v_diff, page_table, c_prompt, c_sample, w_in, w_out, g_attn, g_mlp, w_ada, b_ada, w_up, w_down, lambda_q1, lambda_k1, lambda_q2, lambda_k2, g_subln, g_final):
    b, t, d = x_prompt.shape
    nb, dec_seq, _ = x_sample.shape
    depth = w_in.shape[0]
    n_pool, page = cache_k_moba.shape[1], cache_k_moba.shape[2]
    n_pages = page_table.shape[1]
    assert d == D_MODEL and dec_seq == 1 and t % (2 * MOBA_BLOCK) == 0
    assert MOBA_BLOCK % page == 0 and page == LANES and n_pages * page // MOBA_BLOCK >= MOBA_TOPK
    n_chunk_pages = _pages_per_chunk(n_pages, page)

    to_kt = lambda c: jnp.transpose(c, (0, 1, 3, 4, 2)).reshape(depth, n_pool, GROUP_W, page)
    kc_m, vc_m, kc_d = to_kt(cache_k_moba), to_kt(cache_v_moba), to_kt(cache_k_diff)
    vc_d = cache_v_diff.reshape(depth, n_pool, page * DIFF_HEADS, 2 * HEAD_DIM)

    mod = _ada_call(jnp.concatenate([c_prompt, c_sample], axis=0), w_ada, b_ada)
    _, _, cos_t, sin_t = _rope_tables(jnp.arange(t, dtype=jnp.int32))
    cos_s, sin_s, _, _ = _rope_tables(jnp.full((nb,), n_pages * page, jnp.int32))

    xp = x_prompt
    xs = x_sample.reshape(1, nb, d)
    kt_zero = jnp.zeros((depth, b, GROUP_W, t), F32)
    layered = (jnp.zeros((depth, b, DIFF_HEADS * t, LANES), F32), kt_zero, kt_zero, kt_zero)
    sample_new = []
    for l in range(depth):
        lam_init = 0.8 - 0.6 * math.exp(-0.3 * l)
        final = l == depth - 1
        lamv = jnp.stack([lambda_q1[l], lambda_k1[l], lambda_q2[l], lambda_k2[l]])
        g_sub = g_subln[l].reshape(1, -1)
        g_a, g_m = g_attn[l].reshape(1, d), g_mlp[l].reshape(1, d)
        g_f = g_final.reshape(1, d)
        mods_p = [m.reshape(b, 1, d) for m in jnp.split(mod[l, :b], 6, axis=-1)]
        mods_s = [m.reshape(1, nb, d) for m in jnp.split(mod[l, b:], 6, axis=-1)]
        win = w_in[l]
        groups = [win[:, k * GROUP_W:(k + 1) * GROUP_W] for k in range(6)]
        wa = groups[5].astype(BF16)
        wbt = jnp.concatenate(groups[:5], axis=1).T.astype(BF16)

        sh_a, sc_a, gt_a, sh_m, sc_m, gt_m = mods_p
        qm, qd, vd_all, ktm_all, vtm_all, ktd_all = _proj_call(
            l, depth, xp, g_a, sc_a, sh_a, wa, wbt, cos_t, sin_t, layered)
        layered = (vd_all, ktm_all, vtm_all, ktd_all)
        om = _moba_call(l, qm, ktm_all, vtm_all)
        od = _diff_call(l, lam_init, lamv, g_sub, qd, ktd_all, vd_all)
        x1, h2 = _outproj_call(f"outproj_l{l}", xp, om, od, w_out[l].astype(BF16), gt_a, g_m, sc_m, sh_m,
                               min(PROJ_TM, t), False)
        xp = _mlp_call(f"mlp_l{l}", h2, x1, w_up[l].astype(BF16), w_down[l].astype(BF16), gt_m, g_f,
                       min(MLP_TM, t), False, final)

        sh_a, sc_a, gt_a, sh_m, sc_m, gt_m = mods_s
        ps = _sproj_call(l, xs[0], g_a, sc_a[0], sh_a[0], win, cos_s, sin_s)
        qm_s, km_s, vm_s, qd_s, kd_s, vd_s = [ps[:, k * GROUP_W:(k + 1) * GROUP_W] for k in range(6)]
        sample_new.append((km_s, vm_s, kd_s, vd_s))
        qrep_m = _lane_rep(qm_s)
        sel = _dec_gate_call(l, page_table, qrep_m, kc_m, n_chunk_pages)[:, :, :, 0]
        om_s = _dec_moba_call(l, page_table, sel.reshape(nb, MOBA_TOPK * MOBA_HEADS), qrep_m, _lane_rep(km_s),
                              _lane_rep(vm_s), kc_m, vc_m)[:, :, 0]
        od_s = _dec_diff_call(l, lam_init, page_table, lamv, g_sub, _lane_rep(qd_s), _lane_rep(kd_s),
                              vd_s.reshape(nb, DIFF_HEADS, 2 * HEAD_DIM), kc_d, vc_d, n_chunk_pages)
        x1s, h2s = _outproj_call(f"outproj_s_l{l}", xs, om_s.reshape(1, nb, GROUP_W), od_s.reshape(1, nb, GROUP_W),
                                 w_out[l], gt_a, g_m, sc_m, sh_m, nb, True)
        xs = _mlp_call(f"mlp_s_l{l}", h2s, x1s, w_up[l], w_down[l], gt_m, g_f, nb, True, final)

    vd_all, ktm_all, vtm_all, ktd_all = layered
    from_kt = lambda a: jnp.transpose(a.reshape(depth, b, GROUP_W // HEAD_DIM, HEAD_DIM, t), (0, 1, 4, 2, 3))
    stack_s = lambda k, shape: jnp.stack([s[k] for s in sample_new]).reshape((depth, nb, 1) + shape)
    return (
        xp,
        xs.reshape(nb, 1, d),
        from_kt(ktm_all), from_kt(vtm_all), from_kt(ktd_all),
        vd_all.reshape(depth, b, t, DIFF_HEADS, 2 * HEAD_DIM),
        stack_s(0, (MOBA_HEADS, HEAD_DIM)), stack_s(1, (MOBA_HEADS, HEAD_DIM)),
        stack_s(2, (2 * DIFF_HEADS, HEAD_DIM)), stack_s(3, (DIFF_HEADS, 2 * HEAD_DIM)),
    )
```

```python
import functools
import math

import jax
import jax.numpy as jnp
from jax import lax
from jax.experimental import pallas as pl
from jax.experimental.pallas import tpu as pltpu

F32 = jnp.float32
BF16 = jnp.bfloat16
HIGHEST = lax.Precision.HIGHEST

D_MODEL = 1024
HEAD_DIM = 64
HALF_DIM = HEAD_DIM // 2
MOBA_HEADS = 8
DIFF_HEADS = 4
GROUP_W = 512
MOBA_BLOCK = 256
MOBA_TOPK = 3
ATTN_TQ = 2 * MOBA_BLOCK
D_FF = 4 * D_MODEL
ROPE_THETA = 10000.0
EPS = 1e-6
NEG = -1e30
Q_SCALE = 1.0 / math.sqrt(HEAD_DIM)
Q_SCALE_LOG2 = Q_SCALE * math.log2(math.e)

LANES = 128
SUBLANES = 8
BF16_ROWS = 16
VMEM_LIMIT_BYTES = 52 * 1024 * 1024

PROJ_TM = 512
MLP_TM = 1024
MLP_TF = 1024
ADA_TN = 1536
DEC_GROUP_PAGES = 4
MAX_CHUNK_PAGES = 16


def _cparams(semantics):
    return pltpu.CompilerParams(dimension_semantics=semantics, vmem_limit_bytes=VMEM_LIMIT_BYTES)


def _rmsnorm(x, g):
    return x * lax.rsqrt(jnp.mean(x * x, axis=-1, keepdims=True) + EPS) * g


def _dot(a, b, exact):
    if exact:
        return jnp.dot(a.astype(F32), b.astype(F32), precision=HIGHEST, preferred_element_type=F32)
    return jnp.dot(a.astype(BF16), b.astype(BF16), preferred_element_type=F32)


def _dot_nt(a, b):
    return lax.dot_general(a, b, (((1,), (1,)), ((), ())), preferred_element_type=F32)


def _swap_halves(x):
    lane = lax.broadcasted_iota(jnp.int32, x.shape, 1)
    first_half = (lane % HEAD_DIM) < HALF_DIM
    return jnp.where(first_half, pltpu.roll(x, LANES - HALF_DIM, 1), pltpu.roll(x, HALF_DIM, 1))


def _rope_rows(p, cos, sin_signed):
    return p * cos + _swap_halves(p) * sin_signed


def _ada_kernel(c_ref, w_ref, b_ref, o_ref):
    c = c_ref[...]
    silu = c / (1.0 + jnp.exp(-c))
    o_ref[0] = _dot(silu, w_ref[0], True) + b_ref[0]


def _ada_call(c_all, w_ada, b_ada):
    depth, d, n = w_ada.shape
    rows = c_all.shape[0]
    return pl.pallas_call(
        _ada_kernel,
        grid=(depth, n // ADA_TN),
        in_specs=[
            pl.BlockSpec((rows, d), lambda l, j: (0, 0)),
            pl.BlockSpec((1, d, ADA_TN), lambda l, j: (l, 0, j)),
            pl.BlockSpec((1, 1, ADA_TN), lambda l, j: (l, 0, j)),
        ],
        out_specs=pl.BlockSpec((1, rows, ADA_TN), lambda l, j: (l, 0, j)),
        out_shape=jax.ShapeDtypeStruct((depth, rows, n), F32),
        compiler_params=_cparams(("parallel", "parallel")),
        name="adaln",
    )(c_all, w_ada, b_ada.reshape(depth, 1, n))


def _proj_kernel(x_ref, g_ref, sc_ref, sh_ref, wa_ref, wbt_ref, ct_ref, st_ref, *rest):
    qmt_ref, qdt_ref, vd_ref, ktm_ref, vtm_ref, ktd_ref = rest[-6:]
    tm = x_ref.shape[1]
    h = (_rmsnorm(x_ref[0], g_ref[...]) * (1.0 + sc_ref[0]) + sh_ref[0]).astype(BF16)

    pa = jnp.dot(h, wa_ref[...], preferred_element_type=F32)
    for hd in range(DIFF_HEADS):
        vd_ref[0, 0, pl.ds(hd, tm, stride=DIFF_HEADS), :] = pa[:, hd * LANES:(hd + 1) * LANES]

    pb = _dot_nt(wbt_ref[...], h)
    cos_t = ct_ref[...]
    sin_t = st_ref[...]
    vtm_ref[0, 0] = pb[2 * GROUP_W:3 * GROUP_W]
    for src, dst, scale in ((0, qmt_ref, Q_SCALE_LOG2), (GROUP_W, ktm_ref, None),
                            (3 * GROUP_W, qdt_ref, Q_SCALE_LOG2), (4 * GROUP_W, ktd_ref, None)):
        for hd in range(GROUP_W // HEAD_DIM):
            r0 = src + hd * HEAD_DIM
            x1 = pb[r0:r0 + HALF_DIM]
            x2 = pb[r0 + HALF_DIM:r0 + HEAD_DIM]
            y1 = x1 * cos_t - x2 * sin_t
            y2 = x1 * sin_t + x2 * cos_t
            o0 = hd * HEAD_DIM
            if scale is None:
                dst[0, 0, o0:o0 + HALF_DIM, :] = y1
                dst[0, 0, o0 + HALF_DIM:o0 + HEAD_DIM, :] = y2
            else:
                dst[0, o0:o0 + HALF_DIM, :] = (y1 * scale).astype(dst.dtype)
                dst[0, o0 + HALF_DIM:o0 + HEAD_DIM, :] = (y2 * scale).astype(dst.dtype)


def _proj_call(layer, depth, x, g, sc, sh, wa, wbt, cos_t, sin_t, layered):
    b, t, d = x.shape
    tm = min(PROJ_TM, t)
    in_specs = [
        pl.BlockSpec((1, tm, d), lambda i, j: (i, j, 0)),
        pl.BlockSpec((1, d), lambda i, j: (0, 0)),
        pl.BlockSpec((1, 1, d), lambda i, j: (i, 0, 0)),
        pl.BlockSpec((1, 1, d), lambda i, j: (i, 0, 0)),
        pl.BlockSpec(wa.shape, lambda i, j: (0, 0)),
        pl.BlockSpec(wbt.shape, lambda i, j: (0, 0)),
        pl.BlockSpec((HALF_DIM, tm), lambda i, j: (0, j)),
        pl.BlockSpec((HALF_DIM, tm), lambda i, j: (0, j)),
    ]
    args = [x, g, sc, sh, wa, wbt, cos_t, sin_t]
    aliases = {}
    for k, arr in enumerate(layered):
        in_specs.append(pl.BlockSpec(memory_space=pl.ANY))
        aliases[len(args)] = 2 + k
        args.append(arr)
    kt_shape = jax.ShapeDtypeStruct((depth, b, GROUP_W, t), F32)
    kt_spec = pl.BlockSpec((1, 1, GROUP_W, tm), lambda i, j: (layer, i, 0, j))
    qt_spec = pl.BlockSpec((1, GROUP_W, tm), lambda i, j: (i, 0, j))
    return pl.pallas_call(
        _proj_kernel,
        grid=(b, t // tm),
        in_specs=in_specs,
        out_specs=[
            qt_spec, qt_spec,
            pl.BlockSpec((1, 1, DIFF_HEADS * tm, LANES), lambda i, j: (layer, i, j, 0)),
            kt_spec, kt_spec, kt_spec,
        ],
        out_shape=[
            jax.ShapeDtypeStruct((b, GROUP_W, t), BF16),
            jax.ShapeDtypeStruct((b, GROUP_W, t), BF16),
            jax.ShapeDtypeStruct((depth, b, DIFF_HEADS * t, LANES), F32),
            kt_shape, kt_shape, kt_shape,
        ],
        input_output_aliases=aliases,
        compiler_params=_cparams(("parallel", "parallel")),
        name=f"proj_l{layer}",
    )(*args)


def _tile_rows(row, n_rows):
    return jnp.concatenate([row] * (n_rows // SUBLANES), axis=0)


def _stash_scores(scores, slot, s_s, mc_s):
    for idx, s in enumerate(scores):
        s_s[slot, idx] = s
        mc_s[slot, idx] = jnp.broadcast_to(jnp.max(s, axis=0, keepdims=True), mc_s.shape[2:])


def _fold_scores(slot, values, s_s, mc_s, m_s, l_s, acc_s):
    for idx, vt in enumerate(values):
        s = s_s[slot, idx]
        m_prev = m_s[idx]
        m_next = jnp.maximum(m_prev, mc_s[slot, idx])
        alpha = jnp.exp2(m_prev - m_next)
        p = jnp.exp2(s - _tile_rows(m_next, s.shape[0]))
        l_s[idx] = alpha * l_s[idx] + jnp.broadcast_to(jnp.sum(p, axis=0, keepdims=True), alpha.shape)
        acc_s[idx] = (acc_s[idx] * _tile_rows(alpha, vt.shape[0])
                      + jnp.dot(vt, p.astype(BF16), preferred_element_type=F32))
        m_s[idx] = m_next


def _attend(qi, scores_fn, values_fn, s_s, mc_s, m_s, l_s, acc_s):
    m_s[...] = jnp.full(m_s.shape, -jnp.inf, F32)
    l_s[...] = jnp.zeros_like(l_s)
    acc_s[...] = jnp.zeros_like(acc_s)
    for k in range(2):
        _stash_scores(scores_fn(2 * qi + k, True), k, s_s, mc_s)

    def fold_pair(base, first_tile):
        for k in range(2):
            _fold_scores(base + k, values_fn(first_tile + k), s_s, mc_s, m_s, l_s, acc_s)

    def body(it, carry):
        base = 2 * lax.rem(it, 2)
        ahead = [scores_fn(2 * it + k, False) for k in range(2)]
        fold_pair(base, jnp.where(it == 0, 2 * qi, 2 * it - 2))
        for k in range(2):
            _stash_scores(ahead[k], 2 - base + k, s_s, mc_s)
        return carry

    lax.fori_loop(0, qi, body, 0)
    fold_pair(2 * lax.rem(qi, 2), jnp.where(qi == 0, 0, 2 * qi - 2))


def _split_heads(qt):
    zero = jnp.zeros((HEAD_DIM, qt.shape[1]), qt.dtype)
    return (jnp.concatenate([qt[:HEAD_DIM], zero], axis=0), jnp.concatenate([zero, qt[HEAD_DIM:]], axis=0))


def _causal_tile(tile, qi, n_keys, tq):
    key = tile * n_keys + lax.broadcasted_iota(jnp.int32, (n_keys, tq), 0)
    qry = qi * tq + lax.broadcasted_iota(jnp.int32, (n_keys, tq), 1)
    return key <= qry


def _top_blocks_t(gate, q_blk, n_blk):
    blk = lax.broadcasted_iota(jnp.int32, gate.shape, 0)
    past = blk < q_blk
    g = jnp.where(past, gate, -jnp.inf)
    rank = jnp.zeros(gate.shape, F32)
    for m in range(n_blk):
        gm = jnp.broadcast_to(g[m:m + 1, :], gate.shape)
        beats = (gm > g) | ((gm == g) & (m < blk))
        rank = rank + jnp.where(beats, 1.0, 0.0)
    keep = (past & (rank < float(MOBA_TOPK))) | (blk == q_blk)
    return jnp.where(keep, 1.0, 0.0)


def _moba_kernel(*refs, dec):
    if dec is None:
        qt_ref, kt_ref, vt_ref, o_ref, *scratch = refs
    else:
        pt_ref, qt_ref, kt_ref, vt_ref, qrep_ref, kc_ref, o_ref, sel_ref, *scratch = refs
        *scratch, buf, sem, g_s = scratch
        _dec_gate_step(pt_ref, qrep_ref, kc_ref, sel_ref, buf, sem, g_s, layer=dec[0], n=dec[1],
                       pos=_flat_pos(dec[2]))
    k_s, vt_s, kmh_s, kml_s, sel_s, s_s, mc_s, m_s, l_s, acc_s = scratch
    qi = pl.program_id(2)
    n_blk, blk, _ = k_s.shape
    tq = qt_ref.shape[2]

    @pl.when(qi == 0)
    def _stage():
        row = lax.broadcasted_iota(jnp.int32, kmh_s.shape, 0)
        km = jnp.zeros(kmh_s.shape, F32)
        for n in range(n_blk):
            k = kt_ref[0, 0, :, n * blk:(n + 1) * blk].T
            k_s[n] = k.astype(BF16)
            vt_s[n] = vt_ref[0, 0, :, n * blk:(n + 1) * blk].astype(BF16)
            km = jnp.where(row == n, jnp.sum(k, axis=0, keepdims=True) * (1.0 / blk), km)
        hi = km.astype(BF16)
        kmh_s[...] = hi
        kml_s[...] = (km - hi.astype(F32)).astype(BF16)

    q_heads = _split_heads(qt_ref[0])
    lane = lax.broadcasted_iota(jnp.int32, sel_s.shape[1:], 1)
    q_blk = qi * (tq // blk)
    for k in range(1, tq // blk):
        q_blk = q_blk + jnp.where(lane >= k * blk, 1, 0)
    for idx, qh in enumerate(q_heads):
        gate = (jnp.dot(kmh_s[...], qh, preferred_element_type=F32)
                + jnp.dot(kml_s[...], qh, preferred_element_type=F32))
        sel_s[idx] = _top_blocks_t(gate, q_blk, n_blk)

    def scores(tile, with_diagonal):
        out = []
        for idx, qh in enumerate(q_heads):
            s = jnp.dot(k_s[tile], qh, preferred_element_type=F32)
            keep = jnp.broadcast_to(sel_s[idx, pl.ds(tile, 1), :], (blk, tq)) > 0.0
            s = jnp.where(keep, s, NEG)
            if with_diagonal:
                s = jnp.where(_causal_tile(tile, qi, blk, tq), s, NEG)
            out.append(s)
        return out

    def values(tile):
        return [vt_s[tile, idx * HEAD_DIM:(idx + 1) * HEAD_DIM, :] for idx in range(2)]

    _attend(qi, scores, values, s_s, mc_s, m_s, l_s, acc_s)
    ot = jnp.concatenate([acc_s[idx] / _tile_rows(l_s[idx], HEAD_DIM) for idx in range(2)], axis=0)
    o_ref[0] = ot.T.astype(o_ref.dtype)


def _attn_scratch(n_tile, n_keys, tq, v_rows):
    return [
        pltpu.VMEM((n_tile, n_keys, LANES), BF16),
        pltpu.VMEM((n_tile, LANES, n_keys), BF16),
        pltpu.VMEM((4, 2, n_keys, tq), F32),
        pltpu.VMEM((4, 2, SUBLANES, tq), F32),
        pltpu.VMEM((2, SUBLANES, tq), F32),
        pltpu.VMEM((2, SUBLANES, tq), F32),
        pltpu.VMEM((2, v_rows, tq), F32),
    ]


def _seq_of_step(grid, n_chunk):
    return lambda i, h, j: ((i * grid[1] + h) * grid[2] + j) // n_chunk


def _moba_call(layer, qt, kt_all, vt_all, gate=None):
    b, _, t = qt.shape
    n_blk = t // MOBA_BLOCK
    tq = ATTN_TQ
    grid = (b, GROUP_W // LANES, t // tq)
    k_s, vt_s, s_s, mc_s, m_s, l_s, acc_s = _attn_scratch(n_blk, MOBA_BLOCK, tq, HEAD_DIM)
    mean_rows = -(-n_blk // BF16_ROWS) * BF16_ROWS
    scratch = [
        k_s, vt_s,
        pltpu.VMEM((mean_rows, LANES), BF16),
        pltpu.VMEM((mean_rows, LANES), BF16),
        pltpu.VMEM((2, mean_rows, tq), F32),
        s_s, mc_s, m_s, l_s, acc_s,
    ]
    in_specs = [pl.BlockSpec((1, LANES, tq), lambda i, h, j, *_: (i, h, j)),
                pl.BlockSpec((1, 1, LANES, t), lambda i, h, j, *_: (layer, i, h, 0)),
                pl.BlockSpec((1, 1, LANES, t), lambda i, h, j, *_: (layer, i, h, 0))]
    out_spec = pl.BlockSpec((1, tq, LANES), lambda i, h, j, *_: (i, j, h))
    out_shape = jax.ShapeDtypeStruct((b, t, GROUP_W), BF16)
    if gate is None:
        return pl.pallas_call(
            functools.partial(_moba_kernel, dec=None),
            grid=grid, in_specs=in_specs, out_specs=out_spec, out_shape=out_shape, scratch_shapes=scratch,
            compiler_params=_cparams(("parallel", "parallel", "arbitrary")),
            name=f"moba_l{layer}",
        )(qt, kt_all, vt_all)
    page_table, qrep, kc, n = gate
    nb, n_pages = page_table.shape
    rows, page = kc.shape[2], kc.shape[3]
    n_chunk = n_pages // n
    seq = _seq_of_step(grid, n_chunk)
    return pl.pallas_call(
        functools.partial(_moba_kernel, dec=(layer, n, n_chunk)),
        grid_spec=pltpu.PrefetchScalarGridSpec(
            num_scalar_prefetch=1,
            grid=grid,
            in_specs=in_specs + [pl.BlockSpec((1, rows, LANES), lambda i, h, j, pt: (seq(i, h, j), 0, 0)),
                                 pl.BlockSpec(memory_space=pl.ANY)],
            out_specs=[out_spec,
                       pl.BlockSpec((1, MOBA_TOPK, MOBA_HEADS, LANES), lambda i, h, j, pt: (seq(i, h, j), 0, 0, 0))],
            scratch_shapes=scratch + [pltpu.VMEM((2, n, rows, page), F32), pltpu.SemaphoreType.DMA((1, 2)),
                                      pltpu.VMEM((n_pages * page // MOBA_BLOCK, MOBA_HEADS, LANES), F32)],
        ),
        out_shape=[out_shape, jax.ShapeDtypeStruct((nb, MOBA_TOPK, MOBA_HEADS, LANES), jnp.int32)],
        compiler_params=_cparams(("arbitrary", "arbitrary", "arbitrary")),
        name=f"moba_gate_l{layer}",
    )(page_table, qt, kt_all, vt_all, qrep, kc)


def _lambda(lam_ref, lam_init):
    a = jnp.sum(lam_ref[0:1, :] * lam_ref[1:2, :], axis=1, keepdims=True)
    b = jnp.sum(lam_ref[2:3, :] * lam_ref[3:4, :], axis=1, keepdims=True)
    return jnp.exp(a) - jnp.exp(b) + lam_init


def _diff_kernel(*refs, lam_init, dec):
    if dec is None:
        lam_ref, g_ref, qt_ref, kt_ref, v_ref, o_ref, *scratch = refs
    else:
        (pt_ref, lam_ref, g_ref, qt_ref, kt_ref, v_ref, qrep_ref, knrep_ref, vnew_ref, kc_ref, vc_ref,
         o_ref, od_ref, *scratch) = refs
        *scratch, kbuf, vbuf, sem, dm_s, dl_s, dacc_s = scratch
        _dec_diff_step(pt_ref, lam_ref, g_ref, qrep_ref, knrep_ref, vnew_ref, kc_ref, vc_ref, od_ref,
                       kbuf, vbuf, sem, dm_s, dl_s, dacc_s, layer=dec[0], n=dec[1], lam_init=lam_init,
                       pos=_flat_pos(dec[2]))
    k_s, vt_s, s_s, mc_s, m_s, l_s, acc_s = scratch
    hd = pl.program_id(1)
    qi = pl.program_id(2)
    n_blk, blk, _ = k_s.shape
    tq = qt_ref.shape[2]

    @pl.when(qi == 0)
    def _stage():
        for n in range(n_blk):
            k_s[n] = kt_ref[0, 0, :, n * blk:(n + 1) * blk].T.astype(BF16)
            v = v_ref[0, 0, pl.ds(n * blk * DIFF_HEADS + hd, blk, stride=DIFF_HEADS), :]
            vt_s[n] = v.T.astype(BF16)

    q_maps = _split_heads(qt_ref[0])

    def scores(tile, with_diagonal):
        out = [jnp.dot(k_s[tile], qm, preferred_element_type=F32) for qm in q_maps]
        if with_diagonal:
            causal = _causal_tile(tile, qi, blk, tq)
            out = [jnp.where(causal, s, NEG) for s in out]
        return out

    def values(tile):
        vt = vt_s[tile]
        return [vt, vt]

    _attend(qi, scores, values, s_s, mc_s, m_s, l_s, acc_s)
    o1, o2 = [acc_s[idx] / _tile_rows(l_s[idx], LANES) for idx in range(2)]
    ot = o1 - _lambda(lam_ref, lam_init) * o2
    ot = ot * lax.rsqrt(jnp.mean(ot * ot, axis=0, keepdims=True) + EPS)
    o_ref[0] = (ot.T * g_ref[...] * (1.0 - lam_init)).astype(o_ref.dtype)


def _diff_call(layer, lam_init, lamv, g_sub, qt, kt_all, v_all, dec=None):
    b, _, t = qt.shape
    tq = ATTN_TQ
    grid = (b, DIFF_HEADS, t // tq)
    in_specs = [
        pl.BlockSpec(lamv.shape, lambda i, h, j, *_: (0, 0)),
        pl.BlockSpec(g_sub.shape, lambda i, h, j, *_: (0, 0)),
        pl.BlockSpec((1, LANES, tq), lambda i, h, j, *_: (i, h, j)),
        pl.BlockSpec((1, 1, LANES, t), lambda i, h, j, *_: (layer, i, h, 0)),
        pl.BlockSpec((1, 1, DIFF_HEADS * t, LANES), lambda i, h, j, *_: (layer, i, 0, 0)),
    ]
    out_spec = pl.BlockSpec((1, tq, LANES), lambda i, h, j, *_: (i, j, h))
    out_shape = jax.ShapeDtypeStruct((b, t, GROUP_W), BF16)
    scratch = _attn_scratch(t // MOBA_BLOCK, MOBA_BLOCK, tq, 2 * HEAD_DIM)
    if dec is None:
        return pl.pallas_call(
            functools.partial(_diff_kernel, lam_init=lam_init, dec=None),
            grid=grid, in_specs=in_specs, out_specs=out_spec, out_shape=out_shape, scratch_shapes=scratch,
            compiler_params=_cparams(("parallel", "parallel", "arbitrary")),
            name=f"diff_l{layer}",
        )(lamv, g_sub, qt, kt_all, v_all)
    page_table, qrep, knrep, vnew, kc, vc, n = dec
    nb, n_pages = page_table.shape
    rows, page = kc.shape[2], kc.shape[3]
    n_chunk = n_pages // n
    n_map = 2 * DIFF_HEADS
    seq = _seq_of_step(grid, n_chunk)
    rep_spec = pl.BlockSpec((1, rows, LANES), lambda i, h, j, pt: (seq(i, h, j), 0, 0))
    head_spec = pl.BlockSpec((1, DIFF_HEADS, LANES), lambda i, h, j, pt: (seq(i, h, j), 0, 0))
    return pl.pallas_call(
        functools.partial(_diff_kernel, lam_init=lam_init, dec=(layer, n, n_chunk)),
        grid_spec=pltpu.PrefetchScalarGridSpec(
            num_scalar_prefetch=1,
            grid=grid,
            in_specs=in_specs + [rep_spec, rep_spec, head_spec,
                                 pl.BlockSpec(memory_space=pl.ANY), pl.BlockSpec(memory_space=pl.ANY)],
            out_specs=[out_spec, head_spec],
            scratch_shapes=scratch + [pltpu.VMEM((2, n, rows, page), F32),
                                      pltpu.VMEM((2, n, vc.shape[2], vc.shape[3]), F32),
                                      pltpu.SemaphoreType.DMA((2, 2)),
                                      pltpu.VMEM((n_map, LANES), F32), pltpu.VMEM((n_map, LANES), F32),
                                      pltpu.VMEM((DIFF_HEADS, n_map, LANES), F32)],
        ),
        out_shape=[out_shape, jax.ShapeDtypeStruct((nb, DIFF_HEADS, LANES), F32)],
        compiler_params=_cparams(("arbitrary", "arbitrary", "arbitrary")),
        name=f"diff_dec_l{layer}",
    )(page_table, lamv, g_sub, qt, kt_all, v_all, qrep, knrep, vnew, kc, vc)


def _outproj_kernel(x_ref, om_ref, od_ref, w_ref, gt_ref, g_ref, sc_ref, sh_ref, x1_ref, h2_ref, *, exact):
    o = _dot(om_ref[0], w_ref[0:GROUP_W, :], exact) + _dot(od_ref[0], w_ref[GROUP_W:2 * GROUP_W, :], exact)
    x1 = x_ref[0] + gt_ref[0] * o
    x1_ref[0] = x1
    h2_ref[0] = (_rmsnorm(x1, g_ref[...]) * (1.0 + sc_ref[0]) + sh_ref[0]).astype(h2_ref.dtype)


def _mod_spec(mod, tm):
    d = mod.shape[-1]
    if mod.shape[1] == 1:
        return pl.BlockSpec((1, 1, d), lambda i, j, *_: (i, 0, 0))
    return pl.BlockSpec((1, tm, d), lambda i, j, *_: (i, j, 0))


def _outproj_call(name, x, om, od, w, gt, g, sc, sh, tm, exact):
    grp, rows, d = x.shape
    row_spec = lambda width: pl.BlockSpec((1, tm, width), lambda i, j: (i, j, 0))
    return pl.pallas_call(
        functools.partial(_outproj_kernel, exact=exact),
        grid=(grp, rows // tm),
        in_specs=[
            row_spec(d), row_spec(GROUP_W), row_spec(GROUP_W),
            pl.BlockSpec(w.shape, lambda i, j: (0, 0)),
            _mod_spec(gt, tm),
            pl.BlockSpec((1, d), lambda i, j: (0, 0)),
            _mod_spec(sc, tm), _mod_spec(sh, tm),
        ],
        out_specs=[row_spec(d), row_spec(d)],
        out_shape=[jax.ShapeDtypeStruct(x.shape, F32), jax.ShapeDtypeStruct(x.shape, F32 if exact else BF16)],
        compiler_params=_cparams(("parallel", "parallel")),
        name=name,
    )(x, om, od, w, gt, g, sc, sh)


def _mlp_kernel(h_ref, x1_ref, wu_ref, wd_ref, gt_ref, gf_ref, o_ref, acc_s, *, exact, final_norm):
    f = pl.program_id(2)

    @pl.when(f == 0)
    def _zero():
        acc_s[...] = jnp.zeros_like(acc_s)

    u = jnp.maximum(_dot(h_ref[0], wu_ref[...], exact), 0.0)
    acc_s[...] += _dot(u * u, wd_ref[...], exact)

    @pl.when(f == pl.num_programs(2) - 1)
    def _finish():
        x2 = x1_ref[0] + gt_ref[0] * acc_s[...]
        o_ref[0] = _rmsnorm(x2, gf_ref[...]) if final_norm else x2


def _mlp_call(name, h, x1, wu, wd, gt, gf, tm, exact, final_norm):
    grp, rows, d = x1.shape
    ff = wu.shape[1]
    tf = min(MLP_TF, ff)
    row_spec = pl.BlockSpec((1, tm, d), lambda i, j, f: (i, j, 0))
    return pl.pallas_call(
        functools.partial(_mlp_kernel, exact=exact, final_norm=final_norm),
        grid=(grp, rows // tm, ff // tf),
        in_specs=[
            row_spec, row_spec,
            pl.BlockSpec((d, tf), lambda i, j, f: (0, f)),
            pl.BlockSpec((tf, d), lambda i, j, f: (f, 0)),
            _mod_spec(gt, tm),
            pl.BlockSpec((1, d), lambda i, j, f: (0, 0)),
        ],
        out_specs=row_spec,
        out_shape=jax.ShapeDtypeStruct(x1.shape, F32),
        scratch_shapes=[pltpu.VMEM((tm, d), F32)],
        compiler_params=_cparams(("parallel", "parallel", "arbitrary")),
        name=name,
    )(h, x1, wu, wd, gt, gf)


def _sproj_kernel(x_ref, g_ref, sc_ref, sh_ref, w_ref, cr_ref, sr_ref, o_ref):
    j = pl.program_id(0)
    h = _rmsnorm(x_ref[...], g_ref[...]) * (1.0 + sc_ref[...]) + sh_ref[...]
    p = _dot(h, w_ref[...], True)
    is_q = (j == 0) | (j == 3)
    is_rope = is_q | (j == 1) | (j == 4)
    scale = jnp.where(is_q, Q_SCALE, 1.0)
    cos_r = cr_ref[...]
    sin_r = sr_ref[...]
    for c in range(GROUP_W // LANES):
        pc = p[:, c * LANES:(c + 1) * LANES]
        o_ref[:, c * LANES:(c + 1) * LANES] = jnp.where(is_rope, _rope_rows(pc, cos_r, sin_r), pc) * scale


def _sproj_call(layer, x, g, sc, sh, w, cos_r, sin_r):
    rows, d = x.shape
    n = w.shape[1]
    full = lambda a: pl.BlockSpec(a.shape, lambda j: (0, 0))
    return pl.pallas_call(
        _sproj_kernel,
        grid=(n // GROUP_W,),
        in_specs=[full(x), full(g), full(sc), full(sh), pl.BlockSpec((d, GROUP_W), lambda j: (0, j)),
                  full(cos_r), full(sin_r)],
        out_specs=pl.BlockSpec((rows, GROUP_W), lambda j: (0, j)),
        out_shape=jax.ShapeDtypeStruct((rows, n), F32),
        compiler_params=_cparams(("parallel",)),
        name=f"sproj_l{layer}",
    )(x, g, sc, sh, w, cos_r, sin_r)


def _head_scores(kt, qrep):
    n_heads = kt.shape[0] // HEAD_DIM
    prod = (kt * qrep).reshape(n_heads, HEAD_DIM // SUBLANES, SUBLANES, LANES)
    return jnp.sum(jnp.sum(prod, axis=1), axis=1)


def _chunk_copies(pt_ref, srcs, bufs, sem, layer, seq, chunk, slot, n):
    out = []
    for k, (src, buf) in enumerate(zip(srcs, bufs)):
        for i in range(n):
            page = pt_ref[seq, chunk * n + i]
            out.append(pltpu.make_async_copy(src.at[layer, page], buf.at[slot, i], sem.at[k, slot]))
    return out


def _grid_pos():
    return pl.program_id(0), pl.program_id(1), pl.num_programs(1), pl.num_programs(0)


def _flat_pos(n_chunk):
    step = 0
    total = 1
    for axis in range(3):
        step = step * pl.num_programs(axis) + pl.program_id(axis)
        total = total * pl.num_programs(axis)
    return step // n_chunk, lax.rem(step, n_chunk), n_chunk, total // n_chunk


def _stream_chunks(pt_ref, srcs, bufs, sem, layer, n, pos):
    seq, chunk, n_chunk, n_seq = pos
    step = seq * n_chunk + chunk
    slot = lax.rem(step, 2)

    @pl.when(step == 0)
    def _prime():
        for cp in _chunk_copies(pt_ref, srcs, bufs, sem, layer, 0, 0, 0, n):
            cp.start()

    @pl.when(step + 1 < n_seq * n_chunk)
    def _prefetch():
        wrap = chunk + 1 == n_chunk
        nseq = jnp.where(wrap, seq + 1, seq)
        nchunk = jnp.where(wrap, 0, chunk + 1)
        for cp in _chunk_copies(pt_ref, srcs, bufs, sem, layer, nseq, nchunk, 1 - slot, n):
            cp.start()

    for cp in _chunk_copies(pt_ref, srcs, bufs, sem, layer, seq, chunk, slot, n):
        cp.wait()
    return slot


def _dec_gate_kernel(pt_ref, qrep_ref, kc_ref, sel_ref, buf, sem, g_s, *, layer, n):
    _dec_gate_step(pt_ref, qrep_ref, kc_ref, sel_ref, buf, sem, g_s, layer=layer, n=n, pos=_grid_pos())


def _dec_gate_step(pt_ref, qrep_ref, kc_ref, sel_ref, buf, sem, g_s, *, layer, n, pos):
    chunk, n_chunk = pos[1], pos[2]
    slot = _stream_chunks(pt_ref, (kc_ref,), (buf,), sem, layer, n, pos)
    qrep = qrep_ref[0]
    per_blk = MOBA_BLOCK // buf.shape[3]
    for k in range(n // per_blk):
        pages = buf[slot, per_blk * k]
        for extra in range(1, per_blk):
            pages = pages + buf[slot, per_blk * k + extra]
        g_s[chunk * (n // per_blk) + k] = _head_scores(pages, qrep)

    @pl.when(chunk == n_chunk - 1)
    def _select():
        n_blk = g_s.shape[0]
        flat = g_s[...].reshape(n_blk * MOBA_HEADS, LANES)
        tot = jnp.broadcast_to(jnp.sum(flat, axis=1, keepdims=True), flat.shape).reshape(g_s.shape)
        blk = lax.broadcasted_iota(jnp.int32, g_s.shape, 0).astype(F32)
        for r in range(MOBA_TOPK):
            m = jnp.max(tot, axis=0, keepdims=True)
            first = jnp.min(jnp.where(tot == m, blk, float(n_blk)), axis=0, keepdims=True)
            sel_ref[0, r] = first[0].astype(jnp.int32)
            tot = jnp.where(blk == first, -jnp.inf, tot)


def _dec_gate_call(layer, page_table, qrep, kc, n):
    nb, n_pages = page_table.shape
    rows, page = kc.shape[2], kc.shape[3]
    n_blk = n_pages * page // MOBA_BLOCK
    return pl.pallas_call(
        functools.partial(_dec_gate_kernel, layer=layer, n=n),
        grid_spec=pltpu.PrefetchScalarGridSpec(
            num_scalar_prefetch=1,
            grid=(nb, n_pages // n),
            in_specs=[pl.BlockSpec((1, rows, LANES), lambda i, c, pt: (i, 0, 0)),
                      pl.BlockSpec(memory_space=pl.ANY)],
            out_specs=pl.BlockSpec((1, MOBA_TOPK, MOBA_HEADS, LANES), lambda i, c, pt: (i, 0, 0, 0)),
            scratch_shapes=[pltpu.VMEM((2, n, rows, page), F32), pltpu.SemaphoreType.DMA((1, 2)),
                            pltpu.VMEM((n_blk, MOBA_HEADS, LANES), F32)],
        ),
        out_shape=jax.ShapeDtypeStruct((nb, MOBA_TOPK, MOBA_HEADS, LANES), jnp.int32),
        compiler_params=_cparams(("arbitrary", "arbitrary")),
        name=f"dec_gate_l{layer}",
    )(page_table, qrep, kc)


def _dec_moba_kernel(pt_ref, sel_ref, qrep_ref, knrep_ref, vnrep_ref, kc_ref, vc_ref, o_ref, kbuf, vbuf, sem,
                     *, layer):
    seq = pl.program_id(0)
    slot = lax.rem(seq, 2)
    per_blk = MOBA_BLOCK // kbuf.shape[3]
    per_head = MOBA_TOPK * per_blk

    def copies(s, sl):
        out = []
        for hd in range(MOBA_HEADS):
            rows = pl.ds(hd * HEAD_DIM, HEAD_DIM)
            for r in range(MOBA_TOPK):
                blk = sel_ref[s, r * MOBA_HEADS + hd]
                for pg in range(per_blk):
                    page = pt_ref[s, per_blk * blk + pg]
                    j = hd * per_head + r * per_blk + pg
                    out.append(pltpu.make_async_copy(kc_ref.at[layer, page, rows], kbuf.at[sl, j], sem.at[0, sl]))
                    out.append(pltpu.make_async_copy(vc_ref.at[layer, page, rows], vbuf.at[sl, j], sem.at[1, sl]))
        return out

    @pl.when(seq == 0)
    def _prime():
        for cp in copies(0, 0):
            cp.start()

    @pl.when(seq + 1 < pl.num_programs(0))
    def _prefetch():
        for cp in copies(seq + 1, 1 - slot):
            cp.start()

    for cp in copies(seq, slot):
        cp.wait()

    for hd in range(MOBA_HEADS):
        r0 = hd * HEAD_DIM
        q = qrep_ref[0, r0:r0 + HEAD_DIM, :]
        s_self = jnp.sum(q * knrep_ref[0, r0:r0 + HEAD_DIM, :], axis=0, keepdims=True)
        s_pages = [jnp.sum(kbuf[slot, hd * per_head + j] * q, axis=0, keepdims=True) for j in range(per_head)]
        m = s_self
        for s in s_pages:
            m = jnp.maximum(m, jnp.max(s, axis=1, keepdims=True))
        p_self = jnp.exp(s_self - m)
        l = p_self
        o = vnrep_ref[0, r0:r0 + HEAD_DIM, :] * p_self
        for j, s in enumerate(s_pages):
            p = jnp.exp(s - m)
            l = l + jnp.sum(p, axis=1, keepdims=True)
            o = o + jnp.sum(vbuf[slot, hd * per_head + j] * p, axis=1, keepdims=True)
        o_ref[0, r0:r0 + HEAD_DIM, :] = o / l


def _dec_moba_call(layer, page_table, sel, qrep, knrep, vnrep, kc, vc):
    nb = page_table.shape[0]
    rows, page = kc.shape[2], kc.shape[3]
    n_slab = MOBA_HEADS * MOBA_TOPK * (MOBA_BLOCK // page)
    rep_spec = pl.BlockSpec((1, rows, LANES), lambda i, pt, sl: (i, 0, 0))
    return pl.pallas_call(
        functools.partial(_dec_moba_kernel, layer=layer),
        grid_spec=pltpu.PrefetchScalarGridSpec(
            num_scalar_prefetch=2,
            grid=(nb,),
            in_specs=[rep_spec, rep_spec, rep_spec,
                      pl.BlockSpec(memory_space=pl.ANY), pl.BlockSpec(memory_space=pl.ANY)],
            out_specs=rep_spec,
            scratch_shapes=[pltpu.VMEM((2, n_slab, HEAD_DIM, page), F32),
                            pltpu.VMEM((2, n_slab, HEAD_DIM, page), F32),
                            pltpu.SemaphoreType.DMA((2, 2))],
        ),
        out_shape=jax.ShapeDtypeStruct((nb, rows, LANES), F32),
        compiler_params=_cparams(("arbitrary",)),
        name=f"dec_moba_l{layer}",
    )(page_table, sel, qrep, knrep, vnrep, kc, vc)


def _dec_diff_kernel(pt_ref, lam_ref, g_ref, qrep_ref, knrep_ref, vnew_ref, kc_ref, vc_ref, o_ref,
                     kbuf, vbuf, sem, m_s, l_s, acc_s, *, layer, n, lam_init):
    _dec_diff_step(pt_ref, lam_ref, g_ref, qrep_ref, knrep_ref, vnew_ref, kc_ref, vc_ref, o_ref,
                   kbuf, vbuf, sem, m_s, l_s, acc_s, layer=layer, n=n, lam_init=lam_init, pos=_grid_pos())


def _dec_diff_step(pt_ref, lam_ref, g_ref, qrep_ref, knrep_ref, vnew_ref, kc_ref, vc_ref, o_ref,
                   kbuf, vbuf, sem, m_s, l_s, acc_s, *, layer, n, lam_init, pos):
    chunk, n_chunk = pos[1], pos[2]
    slot = _stream_chunks(pt_ref, (kc_ref, vc_ref), (kbuf, vbuf), sem, layer, n, pos)
    page = kbuf.shape[3]
    qrep = qrep_ref[0]

    @pl.when(chunk == 0)
    def _init():
        m_s[...] = jnp.full(m_s.shape, -jnp.inf, F32)
        l_s[...] = jnp.zeros_like(l_s)
        acc_s[...] = jnp.zeros_like(acc_s)

    for first in range(0, n, DEC_GROUP_PAGES):
        pages = range(first, min(first + DEC_GROUP_PAGES, n))
        s = jnp.concatenate([_head_scores(kbuf[slot, i], qrep) for i in pages], axis=1)
        m_prev = m_s[...]
        m_next = jnp.maximum(m_prev, jnp.max(s, axis=1, keepdims=True))
        alpha = jnp.exp(m_prev - m_next)
        p = jnp.exp(s - jnp.concatenate([m_next] * len(pages), axis=1))
        l_s[...] = alpha * l_s[...] + jnp.sum(p, axis=1, keepdims=True)
        m_s[...] = m_next
        pb = p.astype(BF16)
        for hd in range(DIFF_HEADS):
            v = jnp.concatenate(
                [vbuf[slot, i, pl.ds(hd, page, stride=DIFF_HEADS), :].astype(BF16) for i in pages], axis=0)
            acc_s[hd] = acc_s[hd] * alpha + jnp.dot(pb, v, preferred_element_type=F32)

    @pl.when(chunk == n_chunk - 1)
    def _finish():
        n_map = 2 * DIFF_HEADS
        s_self = jnp.sum((qrep * knrep_ref[0]).reshape(n_map, HEAD_DIM, LANES), axis=1)
        m_prev = m_s[...]
        m_fin = jnp.maximum(m_prev, s_self)
        alpha = jnp.exp(m_prev - m_fin)
        p_self = jnp.exp(s_self - m_fin)
        l_fin = alpha * l_s[...] + p_self
        lam = _lambda(lam_ref, lam_init)
        for hd in range(DIFF_HEADS):
            maps = (acc_s[hd] * alpha + p_self * vnew_ref[0, hd:hd + 1, :]) / l_fin
            o = maps[2 * hd:2 * hd + 1, :] - lam * maps[2 * hd + 1:2 * hd + 2, :]
            o_ref[0, hd:hd + 1, :] = _rmsnorm(o, g_ref[...]) * (1.0 - lam_init)


def _dec_diff_call(layer, lam_init, page_table, lamv, g_sub, qrep, knrep, vnew, kc, vc, n):
    nb, n_pages = page_table.shape
    rows, page = kc.shape[2], kc.shape[3]
    n_map = 2 * DIFF_HEADS
    full = lambda a: pl.BlockSpec(a.shape, lambda i, c, pt: (0, 0))
    rep_spec = pl.BlockSpec((1, rows, LANES), lambda i, c, pt: (i, 0, 0))
    head_spec = pl.BlockSpec((1, DIFF_HEADS, LANES), lambda i, c, pt: (i, 0, 0))
    return pl.pallas_call(
        functools.partial(_dec_diff_kernel, layer=layer, n=n, lam_init=lam_init),
        grid_spec=pltpu.PrefetchScalarGridSpec(
            num_scalar_prefetch=1,
            grid=(nb, n_pages // n),
            in_specs=[full(lamv), full(g_sub), rep_spec, rep_spec, head_spec,
                      pl.BlockSpec(memory_space=pl.ANY), pl.BlockSpec(memory_space=pl.ANY)],
            out_specs=head_spec,
            scratch_shapes=[pltpu.VMEM((2, n, rows, page), F32),
                            pltpu.VMEM((2, n, vc.shape[2], vc.shape[3]), F32),
                            pltpu.SemaphoreType.DMA((2, 2)),
                            pltpu.VMEM((n_map, LANES), F32), pltpu.VMEM((n_map, LANES), F32),
                            pltpu.VMEM((DIFF_HEADS, n_map, LANES), F32)],
        ),
        out_shape=jax.ShapeDtypeStruct((nb, DIFF_HEADS, LANES), F32),
        compiler_params=_cparams(("arbitrary", "arbitrary")),
        name=f"dec_diff_l{layer}",
    )(page_table, lamv, g_sub, qrep, knrep, vnew, kc, vc)


def _rope_tables(pos):
    inv = ROPE_THETA ** (-jnp.arange(HALF_DIM, dtype=F32) / HALF_DIM)
    ang = pos.astype(F32)[:, None] * inv[None, :]
    cos, sin = jnp.cos(ang), jnp.sin(ang)
    reps = LANES // HEAD_DIM
    cos_r = jnp.tile(jnp.concatenate([cos, cos], axis=1), (1, reps))
    sin_r = jnp.tile(jnp.concatenate([-sin, sin], axis=1), (1, reps))
    return cos_r, sin_r, cos.T, sin.T


def _lane_rep(a):
    return jnp.broadcast_to(a[:, :, None], a.shape + (LANES,))


def _pages_per_chunk(n_pages, page):
    per_blk = MOBA_BLOCK // page
    for n in (16, 8, 4, 2):
        if n_pages % n == 0 and n % per_blk == 0:
            return n
    raise ValueError("page count must be a multiple of the pages per MoBA block")


def _fused_chunk_pages(total_pages, attn_steps, n_pages, page):
    per_blk = MOBA_BLOCK // page
    n = total_pages // attn_steps
    ok = (n * attn_steps == total_pages and n > 0 and n_pages % n == 0 and n % per_blk == 0
          and n <= MAX_CHUNK_PAGES and DIFF_HEADS == GROUP_W // LANES)
    return n if ok else None


def kernel(x_prompt, x_sample, cache_k_moba, cache_v_moba, cache_k_diff, cache_v_diff, page_table, c_prompt, c_sample, w_in, w_out, g_attn, g_mlp, w_ada, b_ada, w_up, w_down, lambda_q1, lambda_k1, lambda_q2, lambda_k2, g_subln, g_final):
    b, t, d = x_prompt.shape
    nb, dec_seq, _ = x_sample.shape
    depth = w_in.shape[0]
    n_pool, page = cache_k_moba.shape[1], cache_k_moba.shape[2]
    n_pages = page_table.shape[1]
    assert d == D_MODEL and dec_seq == 1 and t % (2 * MOBA_BLOCK) == 0
    assert MOBA_BLOCK % page == 0 and page == LANES and n_pages * page // MOBA_BLOCK >= MOBA_TOPK
    n_chunk_pages = _pages_per_chunk(n_pages, page)
    fused_pages = _fused_chunk_pages(nb * n_pages, b * (GROUP_W // LANES) * (t // ATTN_TQ), n_pages, page)

    to_kt = lambda c: jnp.transpose(c, (0, 1, 3, 4, 2)).reshape(depth, n_pool, GROUP_W, page)
    kc_m, vc_m, kc_d = to_kt(cache_k_moba), to_kt(cache_v_moba), to_kt(cache_k_diff)
    vc_d = cache_v_diff.reshape(depth, n_pool, page * DIFF_HEADS, 2 * HEAD_DIM)

    mod = _ada_call(jnp.concatenate([c_prompt, c_sample], axis=0), w_ada, b_ada)
    _, _, cos_t, sin_t = _rope_tables(jnp.arange(t, dtype=jnp.int32))
    cos_s, sin_s, _, _ = _rope_tables(jnp.full((nb,), n_pages * page, jnp.int32))

    xp = x_prompt
    xs = x_sample.reshape(1, nb, d)
    kt_zero = jnp.zeros((depth, b, GROUP_W, t), F32)
    layered = (jnp.zeros((depth, b, DIFF_HEADS * t, LANES), F32), kt_zero, kt_zero, kt_zero)
    sample_new = []
    for l in range(depth):
        lam_init = 0.8 - 0.6 * math.exp(-0.3 * l)
        final = l == depth - 1
        lamv = jnp.stack([lambda_q1[l], lambda_k1[l], lambda_q2[l], lambda_k2[l]])
        g_sub = g_subln[l].reshape(1, -1)
        g_a, g_m = g_attn[l].reshape(1, d), g_mlp[l].reshape(1, d)
        g_f = g_final.reshape(1, d)
        mods_p = [m.reshape(b, 1, d) for m in jnp.split(mod[l, :b], 6, axis=-1)]
        mods_s = [m.reshape(1, nb, d) for m in jnp.split(mod[l, b:], 6, axis=-1)]
        win = w_in[l]
        groups = [win[:, k * GROUP_W:(k + 1) * GROUP_W] for k in range(6)]
        wa = groups[5].astype(BF16)
        wbt = jnp.concatenate(groups[:5], axis=1).T.astype(BF16)

        sh_a, sc_a, gt_a, sh_m, sc_m, gt_m = mods_p
        qm, qd, vd_all, ktm_all, vtm_all, ktd_all = _proj_call(
            l, depth, xp, g_a, sc_a, sh_a, wa, wbt, cos_t, sin_t, layered)
        layered = (vd_all, ktm_all, vtm_all, ktd_all)
        ps = _sproj_call(l, xs[0], g_a, mods_s[1][0], mods_s[0][0], win, cos_s, sin_s)
        qm_s, km_s, vm_s, qd_s, kd_s, vd_s = [ps[:, k * GROUP_W:(k + 1) * GROUP_W] for k in range(6)]
        sample_new.append((km_s, vm_s, kd_s, vd_s))
        qrep_m = _lane_rep(qm_s)
        dec_diff_args = (_lane_rep(qd_s), _lane_rep(kd_s), vd_s.reshape(nb, DIFF_HEADS, 2 * HEAD_DIM), kc_d, vc_d)

        if fused_pages is not None:
            om, sel = _moba_call(l, qm, ktm_all, vtm_all, (page_table, qrep_m, kc_m, fused_pages))
            od, od_s = _diff_call(l, lam_init, lamv, g_sub, qd, ktd_all, vd_all,
                                  (page_table,) + dec_diff_args + (fused_pages,))
        else:
            om = _moba_call(l, qm, ktm_all, vtm_all)
            od = _diff_call(l, lam_init, lamv, g_sub, qd, ktd_all, vd_all)
            sel = _dec_gate_call(l, page_table, qrep_m, kc_m, n_chunk_pages)
            od_s = _dec_diff_call(l, lam_init, page_table, lamv, g_sub, *dec_diff_args, n_chunk_pages)
        om_s = _dec_moba_call(l, page_table, sel[:, :, :, 0].reshape(nb, MOBA_TOPK * MOBA_HEADS), qrep_m,
                              _lane_rep(km_s), _lane_rep(vm_s), kc_m, vc_m)[:, :, 0]

        x1, h2 = _outproj_call(f"outproj_l{l}", xp, om, od, w_out[l].astype(BF16), gt_a, g_m, sc_m, sh_m,
                               min(PROJ_TM, t), False)
        xp = _mlp_call(f"mlp_l{l}", h2, x1, w_up[l].astype(BF16), w_down[l].astype(BF16), gt_m, g_f,
                       min(MLP_TM, t), False, final)
        sh_a, sc_a, gt_a, sh_m, sc_m, gt_m = mods_s
        x1s, h2s = _outproj_call(f"outproj_s_l{l}", xs, om_s.reshape(1, nb, GROUP_W), od_s.reshape(1, nb, GROUP_W),
                                 w_out[l], gt_a, g_m, sc_m, sh_m, nb, True)
        xs = _mlp_call(f"mlp_s_l{l}", h2s, x1s, w_up[l], w_down[l], gt_m, g_f, nb, True, final)

    vd_all, ktm_all, vtm_all, ktd_all = layered
    from_kt = lambda a: jnp.transpose(a.reshape(depth, b, GROUP_W // HEAD_DIM, HEAD_DIM, t), (0, 1, 4, 2, 3))
    stack_s = lambda k, shape: jnp.stack([s[k] for s in sample_new]).reshape((depth, nb, 1) + shape)
    return (
        xp,
        xs.reshape(nb, 1, d),
        from_kt(ktm_all), from_kt(vtm_all), from_kt(ktd_all),
        vd_all.reshape(depth, b, t, DIFF_HEADS, 2 * HEAD_DIM),
        stack_s(0, (MOBA_HEADS, HEAD_DIM)), stack_s(1, (MOBA_HEADS, HEAD_DIM)),
        stack_s(2, (2 * DIFF_HEADS, HEAD_DIM)), stack_s(3, (DIFF_HEADS, 2 * HEAD_DIM)),
    )
```

```python
import functools
import math

import jax
import jax.numpy as jnp
from jax import lax
from jax.experimental import pallas as pl
from jax.experimental.pallas import tpu as pltpu

F32 = jnp.float32
BF16 = jnp.bfloat16
HIGHEST = lax.Precision.HIGHEST

D_MODEL = 1024
HEAD_DIM = 64
HALF_DIM = HEAD_DIM // 2
MOBA_HEADS = 8
DIFF_HEADS = 4
GROUP_W = 512
MOBA_BLOCK = 256
MOBA_TOPK = 3
ATTN_TQ = 2 * MOBA_BLOCK
D_FF = 4 * D_MODEL
ROPE_THETA = 10000.0
EPS = 1e-6
NEG = -1e30
Q_SCALE = 1.0 / math.sqrt(HEAD_DIM)
Q_SCALE_LOG2 = Q_SCALE * math.log2(math.e)

LANES = 128
SUBLANES = 8
BF16_ROWS = 16
VMEM_LIMIT_BYTES = 52 * 1024 * 1024

PROJ_TM = 512
MLP_TM = 1024
MLP_TF = 1024
ADA_TN = 1536
DEC_GROUP_PAGES = 4
MAX_CHUNK_PAGES = 16


def _cparams(semantics):
    return pltpu.CompilerParams(dimension_semantics=semantics, vmem_limit_bytes=VMEM_LIMIT_BYTES)


def _rmsnorm(x, g):
    return x * lax.rsqrt(jnp.mean(x * x, axis=-1, keepdims=True) + EPS) * g


def _dot(a, b, exact):
    if exact:
        return jnp.dot(a.astype(F32), b.astype(F32), precision=HIGHEST, preferred_element_type=F32)
    return jnp.dot(a.astype(BF16), b.astype(BF16), preferred_element_type=F32)


def _dot_nt(a, b):
    return lax.dot_general(a, b, (((1,), (1,)), ((), ())), preferred_element_type=F32)


def _swap_halves(x):
    lane = lax.broadcasted_iota(jnp.int32, x.shape, 1)
    first_half = (lane % HEAD_DIM) < HALF_DIM
    return jnp.where(first_half, pltpu.roll(x, LANES - HALF_DIM, 1), pltpu.roll(x, HALF_DIM, 1))


def _rope_rows(p, cos, sin_signed):
    return p * cos + _swap_halves(p) * sin_signed


def _ada_kernel(c_ref, w_ref, b_ref, o_ref):
    c = c_ref[...]
    silu = c / (1.0 + jnp.exp(-c))
    o_ref[0] = _dot(silu, w_ref[0], True) + b_ref[0]


def _ada_call(c_all, w_ada, b_ada):
    depth, d, n = w_ada.shape
    rows = c_all.shape[0]
    return pl.pallas_call(
        _ada_kernel,
        grid=(depth, n // ADA_TN),
        in_specs=[
            pl.BlockSpec((rows, d), lambda l, j: (0, 0)),
            pl.BlockSpec((1, d, ADA_TN), lambda l, j: (l, 0, j)),
            pl.BlockSpec((1, 1, ADA_TN), lambda l, j: (l, 0, j)),
        ],
        out_specs=pl.BlockSpec((1, rows, ADA_TN), lambda l, j: (l, 0, j)),
        out_shape=jax.ShapeDtypeStruct((depth, rows, n), F32),
        compiler_params=_cparams(("parallel", "parallel")),
        name="adaln",
    )(c_all, w_ada, b_ada.reshape(depth, 1, n))


def _proj_kernel(x_ref, g_ref, sc_ref, sh_ref, wa_ref, wbt_ref, ct_ref, st_ref, *rest):
    qmt_ref, qdt_ref, vd_ref, ktm_ref, vtm_ref, ktd_ref = rest[-6:]
    tm = x_ref.shape[1]
    h = (_rmsnorm(x_ref[0], g_ref[...]) * (1.0 + sc_ref[0]) + sh_ref[0]).astype(BF16)

    pa = jnp.dot(h, wa_ref[...], preferred_element_type=F32)
    slots = range(vd_ref.shape[0])
    for hd in range(DIFF_HEADS):
        for slot in slots:
            vd_ref[slot, 0, pl.ds(hd, tm, stride=DIFF_HEADS), :] = pa[:, hd * LANES:(hd + 1) * LANES]

    pb = _dot_nt(wbt_ref[...], h)
    cos_t = ct_ref[...]
    sin_t = st_ref[...]
    for slot in slots:
        vtm_ref[slot, 0] = pb[2 * GROUP_W:3 * GROUP_W]
    for src, dst, scale in ((0, qmt_ref, Q_SCALE_LOG2), (GROUP_W, ktm_ref, None),
                            (3 * GROUP_W, qdt_ref, Q_SCALE_LOG2), (4 * GROUP_W, ktd_ref, None)):
        for hd in range(GROUP_W // HEAD_DIM):
            r0 = src + hd * HEAD_DIM
            x1 = pb[r0:r0 + HALF_DIM]
            x2 = pb[r0 + HALF_DIM:r0 + HEAD_DIM]
            y1 = x1 * cos_t - x2 * sin_t
            y2 = x1 * sin_t + x2 * cos_t
            o0 = hd * HEAD_DIM
            if scale is None:
                for slot in slots:
                    dst[slot, 0, o0:o0 + HALF_DIM, :] = y1
                    dst[slot, 0, o0 + HALF_DIM:o0 + HEAD_DIM, :] = y2
            else:
                dst[0, o0:o0 + HALF_DIM, :] = (y1 * scale).astype(dst.dtype)
                dst[0, o0 + HALF_DIM:o0 + HEAD_DIM, :] = (y2 * scale).astype(dst.dtype)


def _proj_call(layer, depth, x, g, sc, sh, wa, wbt, cos_t, sin_t, layered):
    b, t, d = x.shape
    tm = min(PROJ_TM, t)
    in_specs = [
        pl.BlockSpec((1, tm, d), lambda i, j: (i, j, 0)),
        pl.BlockSpec((1, d), lambda i, j: (0, 0)),
        pl.BlockSpec((1, 1, d), lambda i, j: (i, 0, 0)),
        pl.BlockSpec((1, 1, d), lambda i, j: (i, 0, 0)),
        pl.BlockSpec(wa.shape, lambda i, j: (0, 0)),
        pl.BlockSpec(wbt.shape, lambda i, j: (0, 0)),
        pl.BlockSpec((HALF_DIM, tm), lambda i, j: (0, j)),
        pl.BlockSpec((HALF_DIM, tm), lambda i, j: (0, j)),
    ]
    args = [x, g, sc, sh, wa, wbt, cos_t, sin_t]
    aliases = {}
    for k, arr in enumerate(layered or ()):
        in_specs.append(pl.BlockSpec(memory_space=pl.ANY))
        aliases[len(args)] = 2 + k
        args.append(arr)
    n_slot, first_slot = (depth, 0) if layered is None else (1, layer)
    kt_shape = jax.ShapeDtypeStruct((depth, b, GROUP_W, t), F32)
    kt_spec = pl.BlockSpec((n_slot, 1, GROUP_W, tm), lambda i, j: (first_slot, i, 0, j))
    qt_spec = pl.BlockSpec((1, GROUP_W, tm), lambda i, j: (i, 0, j))
    return pl.pallas_call(
        _proj_kernel,
        grid=(b, t // tm),
        in_specs=in_specs,
        out_specs=[
            qt_spec, qt_spec,
            pl.BlockSpec((n_slot, 1, DIFF_HEADS * tm, LANES), lambda i, j: (first_slot, i, j, 0)),
            kt_spec, kt_spec, kt_spec,
        ],
        out_shape=[
            jax.ShapeDtypeStruct((b, GROUP_W, t), BF16),
            jax.ShapeDtypeStruct((b, GROUP_W, t), BF16),
            jax.ShapeDtypeStruct((depth, b, DIFF_HEADS * t, LANES), F32),
            kt_shape, kt_shape, kt_shape,
        ],
        input_output_aliases=aliases,
        compiler_params=_cparams(("parallel", "parallel")),
        name=f"proj_l{layer}",
    )(*args)


def _tile_rows(row, n_rows):
    return jnp.concatenate([row] * (n_rows // SUBLANES), axis=0)


def _stash_scores(scores, slot, s_s, mc_s):
    for idx, s in enumerate(scores):
        s_s[slot, idx] = s
        mc_s[slot, idx] = jnp.broadcast_to(jnp.max(s, axis=0, keepdims=True), mc_s.shape[2:])


def _fold_scores(slot, values, s_s, mc_s, m_s, l_s, acc_s):
    for idx, vt in enumerate(values):
        s = s_s[slot, idx]
        m_prev = m_s[idx]
        m_next = jnp.maximum(m_prev, mc_s[slot, idx])
        alpha = jnp.exp2(m_prev - m_next)
        p = jnp.exp2(s - _tile_rows(m_next, s.shape[0]))
        l_s[idx] = alpha * l_s[idx] + jnp.broadcast_to(jnp.sum(p, axis=0, keepdims=True), alpha.shape)
        acc_s[idx] = (acc_s[idx] * _tile_rows(alpha, vt.shape[0])
                      + jnp.dot(vt, p.astype(BF16), preferred_element_type=F32))
        m_s[idx] = m_next


def _attend(qi, scores_fn, values_fn, s_s, mc_s, m_s, l_s, acc_s):
    m_s[...] = jnp.full(m_s.shape, -jnp.inf, F32)
    l_s[...] = jnp.zeros_like(l_s)
    acc_s[...] = jnp.zeros_like(acc_s)
    for k in range(2):
        _stash_scores(scores_fn(2 * qi + k, True), k, s_s, mc_s)

    def fold_pair(base, first_tile):
        for k in range(2):
            _fold_scores(base + k, values_fn(first_tile + k), s_s, mc_s, m_s, l_s, acc_s)

    def body(it, carry):
        base = 2 * lax.rem(it, 2)
        ahead = [scores_fn(2 * it + k, False) for k in range(2)]
        fold_pair(base, jnp.where(it == 0, 2 * qi, 2 * it - 2))
        for k in range(2):
            _stash_scores(ahead[k], 2 - base + k, s_s, mc_s)
        return carry

    lax.fori_loop(0, qi, body, 0)
    fold_pair(2 * lax.rem(qi, 2), jnp.where(qi == 0, 0, 2 * qi - 2))


def _split_heads(qt):
    zero = jnp.zeros((HEAD_DIM, qt.shape[1]), qt.dtype)
    return (jnp.concatenate([qt[:HEAD_DIM], zero], axis=0), jnp.concatenate([zero, qt[HEAD_DIM:]], axis=0))


def _causal_tile(tile, qi, n_keys, tq):
    key = tile * n_keys + lax.broadcasted_iota(jnp.int32, (n_keys, tq), 0)
    qry = qi * tq + lax.broadcasted_iota(jnp.int32, (n_keys, tq), 1)
    return key <= qry


def _top_blocks_t(gate, q_blk, n_blk):
    blk = lax.broadcasted_iota(jnp.int32, gate.shape, 0)
    past = blk < q_blk
    g = jnp.where(past, gate, -jnp.inf)
    rank = jnp.zeros(gate.shape, F32)
    for m in range(n_blk):
        gm = jnp.broadcast_to(g[m:m + 1, :], gate.shape)
        beats = (gm > g) | ((gm == g) & (m < blk))
        rank = rank + jnp.where(beats, 1.0, 0.0)
    keep = (past & (rank < float(MOBA_TOPK))) | (blk == q_blk)
    return jnp.where(keep, 1.0, 0.0)


def _moba_kernel(*refs, dec):
    dec_stream = dec_reduce = dec_finish = lambda: None
    if dec is None:
        qt_ref, kt_ref, vt_ref, o_ref, *scratch = refs
    else:
        pt_ref, qt_ref, kt_ref, vt_ref, qrep_ref, kc_ref, o_ref, sel_ref, *scratch = refs
        *scratch, buf, sem, g_s = scratch
        dec_stream, dec_reduce, dec_finish = _dec_gate_step(
            pt_ref, qrep_ref, kc_ref, sel_ref, buf, sem, g_s, layer=dec[0], n=dec[1], pos=_flat_pos(dec[2]))
    dec_stream()
    k_s, vt_s, kmh_s, kml_s, sel_s, s_s, mc_s, m_s, l_s, acc_s = scratch
    qi = pl.program_id(2)
    n_blk, blk, _ = k_s.shape
    tq = qt_ref.shape[2]

    @pl.when(qi == 0)
    def _stage():
        row = lax.broadcasted_iota(jnp.int32, kmh_s.shape, 0)
        km = jnp.zeros(kmh_s.shape, F32)
        for n in range(n_blk):
            k = kt_ref[0, 0, :, n * blk:(n + 1) * blk].T
            k_s[n] = k.astype(BF16)
            vt_s[n] = vt_ref[0, 0, :, n * blk:(n + 1) * blk].astype(BF16)
            km = jnp.where(row == n, jnp.sum(k, axis=0, keepdims=True) * (1.0 / blk), km)
        hi = km.astype(BF16)
        kmh_s[...] = hi
        kml_s[...] = (km - hi.astype(F32)).astype(BF16)

    dec_reduce()
    q_heads = _split_heads(qt_ref[0])
    lane = lax.broadcasted_iota(jnp.int32, sel_s.shape[1:], 1)
    q_blk = qi * (tq // blk)
    for k in range(1, tq // blk):
        q_blk = q_blk + jnp.where(lane >= k * blk, 1, 0)
    for idx, qh in enumerate(q_heads):
        gate = (jnp.dot(kmh_s[...], qh, preferred_element_type=F32)
                + jnp.dot(kml_s[...], qh, preferred_element_type=F32))
        sel_s[idx] = _top_blocks_t(gate, q_blk, n_blk)

    def scores(tile, with_diagonal):
        out = []
        for idx, qh in enumerate(q_heads):
            s = jnp.dot(k_s[tile], qh, preferred_element_type=F32)
            keep = jnp.broadcast_to(sel_s[idx, pl.ds(tile, 1), :], (blk, tq)) > 0.0
            s = jnp.where(keep, s, NEG)
            if with_diagonal:
                s = jnp.where(_causal_tile(tile, qi, blk, tq), s, NEG)
            out.append(s)
        return out

    def values(tile):
        return [vt_s[tile, idx * HEAD_DIM:(idx + 1) * HEAD_DIM, :] for idx in range(2)]

    _attend(qi, scores, values, s_s, mc_s, m_s, l_s, acc_s)
    ot = jnp.concatenate([acc_s[idx] / _tile_rows(l_s[idx], HEAD_DIM) for idx in range(2)], axis=0)
    o_ref[0] = ot.T.astype(o_ref.dtype)
    dec_finish()


def _attn_scratch(n_tile, n_keys, tq, v_rows):
    return [
        pltpu.VMEM((n_tile, n_keys, LANES), BF16),
        pltpu.VMEM((n_tile, LANES, n_keys), BF16),
        pltpu.VMEM((4, 2, n_keys, tq), F32),
        pltpu.VMEM((4, 2, SUBLANES, tq), F32),
        pltpu.VMEM((2, SUBLANES, tq), F32),
        pltpu.VMEM((2, SUBLANES, tq), F32),
        pltpu.VMEM((2, v_rows, tq), F32),
    ]


def _seq_of_step(grid, n_chunk):
    return lambda i, h, j: ((i * grid[1] + h) * grid[2] + j) // n_chunk


def _moba_call(layer, qt, kt_all, vt_all, gate=None):
    b, _, t = qt.shape
    n_blk = t // MOBA_BLOCK
    tq = ATTN_TQ
    grid = (b, GROUP_W // LANES, t // tq)
    k_s, vt_s, s_s, mc_s, m_s, l_s, acc_s = _attn_scratch(n_blk, MOBA_BLOCK, tq, HEAD_DIM)
    mean_rows = -(-n_blk // BF16_ROWS) * BF16_ROWS
    scratch = [
        k_s, vt_s,
        pltpu.VMEM((mean_rows, LANES), BF16),
        pltpu.VMEM((mean_rows, LANES), BF16),
        pltpu.VMEM((2, mean_rows, tq), F32),
        s_s, mc_s, m_s, l_s, acc_s,
    ]
    in_specs = [pl.BlockSpec((1, LANES, tq), lambda i, h, j, *_: (i, h, j)),
                pl.BlockSpec((1, 1, LANES, t), lambda i, h, j, *_: (layer, i, h, 0)),
                pl.BlockSpec((1, 1, LANES, t), lambda i, h, j, *_: (layer, i, h, 0))]
    out_spec = pl.BlockSpec((1, tq, LANES), lambda i, h, j, *_: (i, j, h))
    out_shape = jax.ShapeDtypeStruct((b, t, GROUP_W), BF16)
    if gate is None:
        return pl.pallas_call(
            functools.partial(_moba_kernel, dec=None),
            grid=grid, in_specs=in_specs, out_specs=out_spec, out_shape=out_shape, scratch_shapes=scratch,
            compiler_params=_cparams(("parallel", "parallel", "arbitrary")),
            name=f"moba_l{layer}",
        )(qt, kt_all, vt_all)
    page_table, qrep, kc, n = gate
    nb, n_pages = page_table.shape
    rows, page = kc.shape[2], kc.shape[3]
    n_chunk = n_pages // n
    seq = _seq_of_step(grid, n_chunk)
    return pl.pallas_call(
        functools.partial(_moba_kernel, dec=(layer, n, n_chunk)),
        grid_spec=pltpu.PrefetchScalarGridSpec(
            num_scalar_prefetch=1,
            grid=grid,
            in_specs=in_specs + [pl.BlockSpec((1, rows, LANES), lambda i, h, j, pt: (seq(i, h, j), 0, 0)),
                                 pl.BlockSpec(memory_space=pl.ANY)],
            out_specs=[out_spec,
                       pl.BlockSpec((1, MOBA_TOPK, MOBA_HEADS, LANES), lambda i, h, j, pt: (seq(i, h, j), 0, 0, 0))],
            scratch_shapes=scratch + [pltpu.VMEM((2, n, rows, page), F32), pltpu.SemaphoreType.DMA((1, 2)),
                                      pltpu.VMEM((n_pages * page // MOBA_BLOCK, MOBA_HEADS, LANES), F32)],
        ),
        out_shape=[out_shape, jax.ShapeDtypeStruct((nb, MOBA_TOPK, MOBA_HEADS, LANES), jnp.int32)],
        compiler_params=_cparams(("arbitrary", "arbitrary", "arbitrary")),
        name=f"moba_gate_l{layer}",
    )(page_table, qt, kt_all, vt_all, qrep, kc)


def _lambda(lam_ref, lam_init):
    a = jnp.sum(lam_ref[0:1, :] * lam_ref[1:2, :], axis=1, keepdims=True)
    b = jnp.sum(lam_ref[2:3, :] * lam_ref[3:4, :], axis=1, keepdims=True)
    return jnp.exp(a) - jnp.exp(b) + lam_init


def _diff_kernel(*refs, lam_init, dec):
    dec_stream = dec_reduce = dec_finish = lambda: None
    if dec is None:
        lam_ref, g_ref, qt_ref, kt_ref, v_ref, o_ref, *scratch = refs
    else:
        (pt_ref, lam_ref, g_ref, qt_ref, kt_ref, v_ref, qrep_ref, knrep_ref, vnew_ref, kc_ref, vc_ref,
         o_ref, od_ref, *scratch) = refs
        *scratch, kbuf, vbuf, sem, dm_s, dl_s, dacc_s = scratch
        dec_stream, dec_reduce, dec_finish = _dec_diff_step(
            pt_ref, lam_ref, g_ref, qrep_ref, knrep_ref, vnew_ref, kc_ref, vc_ref, od_ref,
            kbuf, vbuf, sem, dm_s, dl_s, dacc_s, layer=dec[0], n=dec[1], lam_init=lam_init,
            pos=_flat_pos(dec[2]))
    dec_stream()
    k_s, vt_s, s_s, mc_s, m_s, l_s, acc_s = scratch
    hd = pl.program_id(1)
    qi = pl.program_id(2)
    n_blk, blk, _ = k_s.shape
    tq = qt_ref.shape[2]

    @pl.when(qi == 0)
    def _stage():
        for n in range(n_blk):
            k_s[n] = kt_ref[0, 0, :, n * blk:(n + 1) * blk].T.astype(BF16)
            v = v_ref[0, 0, pl.ds(n * blk * DIFF_HEADS + hd, blk, stride=DIFF_HEADS), :]
            vt_s[n] = v.T.astype(BF16)

    dec_reduce()
    q_maps = _split_heads(qt_ref[0])

    def scores(tile, with_diagonal):
        out = [jnp.dot(k_s[tile], qm, preferred_element_type=F32) for qm in q_maps]
        if with_diagonal:
            causal = _causal_tile(tile, qi, blk, tq)
            out = [jnp.where(causal, s, NEG) for s in out]
        return out

    def values(tile):
        vt = vt_s[tile]
        return [vt, vt]

    _attend(qi, scores, values, s_s, mc_s, m_s, l_s, acc_s)
    o1, o2 = [acc_s[idx] / _tile_rows(l_s[idx], LANES) for idx in range(2)]
    ot = o1 - _lambda(lam_ref, lam_init) * o2
    ot = ot * lax.rsqrt(jnp.mean(ot * ot, axis=0, keepdims=True) + EPS)
    o_ref[0] = (ot.T * g_ref[...] * (1.0 - lam_init)).astype(o_ref.dtype)
    dec_finish()


def _diff_call(layer, lam_init, lamv, g_sub, qt, kt_all, v_all, dec=None):
    b, _, t = qt.shape
    tq = ATTN_TQ
    grid = (b, DIFF_HEADS, t // tq)
    in_specs = [
        pl.BlockSpec(lamv.shape, lambda i, h, j, *_: (0, 0)),
        pl.BlockSpec(g_sub.shape, lambda i, h, j, *_: (0, 0)),
        pl.BlockSpec((1, LANES, tq), lambda i, h, j, *_: (i, h, j)),
        pl.BlockSpec((1, 1, LANES, t), lambda i, h, j, *_: (layer, i, h, 0)),
        pl.BlockSpec((1, 1, DIFF_HEADS * t, LANES), lambda i, h, j, *_: (layer, i, 0, 0)),
    ]
    out_spec = pl.BlockSpec((1, tq, LANES), lambda i, h, j, *_: (i, j, h))
    out_shape = jax.ShapeDtypeStruct((b, t, GROUP_W), BF16)
    scratch = _attn_scratch(t // MOBA_BLOCK, MOBA_BLOCK, tq, 2 * HEAD_DIM)
    if dec is None:
        return pl.pallas_call(
            functools.partial(_diff_kernel, lam_init=lam_init, dec=None),
            grid=grid, in_specs=in_specs, out_specs=out_spec, out_shape=out_shape, scratch_shapes=scratch,
            compiler_params=_cparams(("parallel", "parallel", "arbitrary")),
            name=f"diff_l{layer}",
        )(lamv, g_sub, qt, kt_all, v_all)
    page_table, qrep, knrep, vnew, kc, vc, n = dec
    nb, n_pages = page_table.shape
    rows, page = kc.shape[2], kc.shape[3]
    n_chunk = n_pages // n
    n_map = 2 * DIFF_HEADS
    seq = _seq_of_step(grid, n_chunk)
    rep_spec = pl.BlockSpec((1, rows, LANES), lambda i, h, j, pt: (seq(i, h, j), 0, 0))
    head_spec = pl.BlockSpec((1, DIFF_HEADS, LANES), lambda i, h, j, pt: (seq(i, h, j), 0, 0))
    return pl.pallas_call(
        functools.partial(_diff_kernel, lam_init=lam_init, dec=(layer, n, n_chunk)),
        grid_spec=pltpu.PrefetchScalarGridSpec(
            num_scalar_prefetch=1,
            grid=grid,
            in_specs=in_specs + [rep_spec, rep_spec, head_spec,
                                 pl.BlockSpec(memory_space=pl.ANY), pl.BlockSpec(memory_space=pl.ANY)],
            out_specs=[out_spec, head_spec],
            scratch_shapes=scratch + [pltpu.VMEM((2, n, rows, page), F32),
                                      pltpu.VMEM((2, n, vc.shape[2], vc.shape[3]), F32),
                                      pltpu.SemaphoreType.DMA((2, 2)),
                                      pltpu.VMEM((n_map, LANES), F32), pltpu.VMEM((n_map, LANES), F32),
                                      pltpu.VMEM((DIFF_HEADS, n_map, LANES), F32)],
        ),
        out_shape=[out_shape, jax.ShapeDtypeStruct((nb, DIFF_HEADS, LANES), F32)],
        compiler_params=_cparams(("arbitrary", "arbitrary", "arbitrary")),
        name=f"diff_dec_l{layer}",
    )(page_table, lamv, g_sub, qt, kt_all, v_all, qrep, knrep, vnew, kc, vc)


def _outproj_kernel(x_ref, om_ref, od_ref, w_ref, gt_ref, g_ref, sc_ref, sh_ref, x1_ref, h2_ref, *, exact):
    o = _dot(om_ref[0], w_ref[0:GROUP_W, :], exact) + _dot(od_ref[0], w_ref[GROUP_W:2 * GROUP_W, :], exact)
    x1 = x_ref[0] + gt_ref[0] * o
    x1_ref[0] = x1
    h2_ref[0] = (_rmsnorm(x1, g_ref[...]) * (1.0 + sc_ref[0]) + sh_ref[0]).astype(h2_ref.dtype)


def _mod_spec(mod, tm):
    d = mod.shape[-1]
    if mod.shape[1] == 1:
        return pl.BlockSpec((1, 1, d), lambda i, j, *_: (i, 0, 0))
    return pl.BlockSpec((1, tm, d), lambda i, j, *_: (i, j, 0))


def _outproj_call(name, x, om, od, w, gt, g, sc, sh, tm, exact):
    grp, rows, d = x.shape
    row_spec = lambda width: pl.BlockSpec((1, tm, width), lambda i, j: (i, j, 0))
    return pl.pallas_call(
        functools.partial(_outproj_kernel, exact=exact),
        grid=(grp, rows // tm),
        in_specs=[
            row_spec(d), row_spec(GROUP_W), row_spec(GROUP_W),
            pl.BlockSpec(w.shape, lambda i, j: (0, 0)),
            _mod_spec(gt, tm),
            pl.BlockSpec((1, d), lambda i, j: (0, 0)),
            _mod_spec(sc, tm), _mod_spec(sh, tm),
        ],
        out_specs=[row_spec(d), row_spec(d)],
        out_shape=[jax.ShapeDtypeStruct(x.shape, F32), jax.ShapeDtypeStruct(x.shape, F32 if exact else BF16)],
        compiler_params=_cparams(("parallel", "parallel")),
        name=name,
    )(x, om, od, w, gt, g, sc, sh)


def _mlp_kernel(h_ref, x1_ref, wu_ref, wd_ref, gt_ref, gf_ref, o_ref, acc_s, *, exact, final_norm):
    f = pl.program_id(2)

    @pl.when(f == 0)
    def _zero():
        acc_s[...] = jnp.zeros_like(acc_s)

    u = jnp.maximum(_dot(h_ref[0], wu_ref[...], exact), 0.0)
    acc_s[...] += _dot(u * u, wd_ref[...], exact)

    @pl.when(f == pl.num_programs(2) - 1)
    def _finish():
        x2 = x1_ref[0] + gt_ref[0] * acc_s[...]
        o_ref[0] = _rmsnorm(x2, gf_ref[...]) if final_norm else x2


def _mlp_call(name, h, x1, wu, wd, gt, gf, tm, exact, final_norm):
    grp, rows, d = x1.shape
    ff = wu.shape[1]
    tf = min(MLP_TF, ff)
    row_spec = pl.BlockSpec((1, tm, d), lambda i, j, f: (i, j, 0))
    return pl.pallas_call(
        functools.partial(_mlp_kernel, exact=exact, final_norm=final_norm),
        grid=(grp, rows // tm, ff // tf),
        in_specs=[
            row_spec, row_spec,
            pl.BlockSpec((d, tf), lambda i, j, f: (0, f)),
            pl.BlockSpec((tf, d), lambda i, j, f: (f, 0)),
            _mod_spec(gt, tm),
            pl.BlockSpec((1, d), lambda i, j, f: (0, 0)),
        ],
        out_specs=row_spec,
        out_shape=jax.ShapeDtypeStruct(x1.shape, F32),
        scratch_shapes=[pltpu.VMEM((tm, d), F32)],
        compiler_params=_cparams(("parallel", "parallel", "arbitrary")),
        name=name,
    )(h, x1, wu, wd, gt, gf)


def _sproj_kernel(x_ref, g_ref, sc_ref, sh_ref, w_ref, cr_ref, sr_ref, o_ref):
    j = pl.program_id(0)
    h = _rmsnorm(x_ref[...], g_ref[...]) * (1.0 + sc_ref[...]) + sh_ref[...]
    p = _dot(h, w_ref[...], True)
    is_q = (j == 0) | (j == 3)
    is_rope = is_q | (j == 1) | (j == 4)
    scale = jnp.where(is_q, Q_SCALE, 1.0)
    cos_r = cr_ref[...]
    sin_r = sr_ref[...]
    for c in range(GROUP_W // LANES):
        pc = p[:, c * LANES:(c + 1) * LANES]
        o_ref[:, c * LANES:(c + 1) * LANES] = jnp.where(is_rope, _rope_rows(pc, cos_r, sin_r), pc) * scale


def _sproj_call(layer, x, g, sc, sh, w, cos_r, sin_r):
    rows, d = x.shape
    n = w.shape[1]
    full = lambda a: pl.BlockSpec(a.shape, lambda j: (0, 0))
    return pl.pallas_call(
        _sproj_kernel,
        grid=(n // GROUP_W,),
        in_specs=[full(x), full(g), full(sc), full(sh), pl.BlockSpec((d, GROUP_W), lambda j: (0, j)),
                  full(cos_r), full(sin_r)],
        out_specs=pl.BlockSpec((rows, GROUP_W), lambda j: (0, j)),
        out_shape=jax.ShapeDtypeStruct((rows, n), F32),
        compiler_params=_cparams(("parallel",)),
        name=f"sproj_l{layer}",
    )(x, g, sc, sh, w, cos_r, sin_r)


def _head_scores(kt, qrep):
    n_heads = kt.shape[0] // HEAD_DIM
    prod = (kt * qrep).reshape(n_heads, HEAD_DIM // SUBLANES, SUBLANES, LANES)
    return jnp.sum(jnp.sum(prod, axis=1), axis=1)


def _chunk_copies(pt_ref, srcs, bufs, sem, layer, seq, chunk, slot, n):
    out = []
    for k, (src, buf) in enumerate(zip(srcs, bufs)):
        for i in range(n):
            page = pt_ref[seq, chunk * n + i]
            out.append(pltpu.make_async_copy(src.at[layer, page], buf.at[slot, i], sem.at[k, slot]))
    return out


def _grid_pos():
    return pl.program_id(0), pl.program_id(1), pl.num_programs(1), pl.num_programs(0)


def _flat_pos(n_chunk):
    step = 0
    total = 1
    for axis in range(3):
        step = step * pl.num_programs(axis) + pl.program_id(axis)
        total = total * pl.num_programs(axis)
    return step // n_chunk, lax.rem(step, n_chunk), n_chunk, total // n_chunk


def _stream_chunks(pt_ref, srcs, bufs, sem, layer, n, pos):
    seq, chunk, n_chunk, n_seq = pos
    step = seq * n_chunk + chunk
    slot = lax.rem(step, 2)

    @pl.when(step == 0)
    def _prime():
        for cp in _chunk_copies(pt_ref, srcs, bufs, sem, layer, 0, 0, 0, n):
            cp.start()

    @pl.when(step + 1 < n_seq * n_chunk)
    def _prefetch():
        wrap = chunk + 1 == n_chunk
        nseq = jnp.where(wrap, seq + 1, seq)
        nchunk = jnp.where(wrap, 0, chunk + 1)
        for cp in _chunk_copies(pt_ref, srcs, bufs, sem, layer, nseq, nchunk, 1 - slot, n):
            cp.start()

    for cp in _chunk_copies(pt_ref, srcs, bufs, sem, layer, seq, chunk, slot, n):
        cp.wait()
    return slot


def _dec_gate_kernel(pt_ref, qrep_ref, kc_ref, sel_ref, buf, sem, g_s, *, layer, n):
    for part in _dec_gate_step(pt_ref, qrep_ref, kc_ref, sel_ref, buf, sem, g_s, layer=layer, n=n, pos=_grid_pos()):
        part()


def _dec_gate_step(pt_ref, qrep_ref, kc_ref, sel_ref, buf, sem, g_s, *, layer, n, pos):
    chunk, n_chunk = pos[1], pos[2]
    ready = []

    def stream():
        ready.append(_stream_chunks(pt_ref, (kc_ref,), (buf,), sem, layer, n, pos))

    def reduce():
        slot = ready[0]
        qrep = qrep_ref[0]
        per_blk = MOBA_BLOCK // buf.shape[3]
        for k in range(n // per_blk):
            pages = buf[slot, per_blk * k]
            for extra in range(1, per_blk):
                pages = pages + buf[slot, per_blk * k + extra]
            g_s[chunk * (n // per_blk) + k] = _head_scores(pages, qrep)

    def finish():
        @pl.when(chunk == n_chunk - 1)
        def _select():
            n_blk = g_s.shape[0]
            flat = g_s[...].reshape(n_blk * MOBA_HEADS, LANES)
            tot = jnp.broadcast_to(jnp.sum(flat, axis=1, keepdims=True), flat.shape).reshape(g_s.shape)
            blk = lax.broadcasted_iota(jnp.int32, g_s.shape, 0).astype(F32)
            for r in range(MOBA_TOPK):
                m = jnp.max(tot, axis=0, keepdims=True)
                first = jnp.min(jnp.where(tot == m, blk, float(n_blk)), axis=0, keepdims=True)
                sel_ref[0, r] = first[0].astype(jnp.int32)
                tot = jnp.where(blk == first, -jnp.inf, tot)

    return stream, reduce, finish


def _dec_gate_call(layer, page_table, qrep, kc, n):
    nb, n_pages = page_table.shape
    rows, page = kc.shape[2], kc.shape[3]
    n_blk = n_pages * page // MOBA_BLOCK
    return pl.pallas_call(
        functools.partial(_dec_gate_kernel, layer=layer, n=n),
        grid_spec=pltpu.PrefetchScalarGridSpec(
            num_scalar_prefetch=1,
            grid=(nb, n_pages // n),
            in_specs=[pl.BlockSpec((1, rows, LANES), lambda i, c, pt: (i, 0, 0)),
                      pl.BlockSpec(memory_space=pl.ANY)],
            out_specs=pl.BlockSpec((1, MOBA_TOPK, MOBA_HEADS, LANES), lambda i, c, pt: (i, 0, 0, 0)),
            scratch_shapes=[pltpu.VMEM((2, n, rows, page), F32), pltpu.SemaphoreType.DMA((1, 2)),
                            pltpu.VMEM((n_blk, MOBA_HEADS, LANES), F32)],
        ),
        out_shape=jax.ShapeDtypeStruct((nb, MOBA_TOPK, MOBA_HEADS, LANES), jnp.int32),
        compiler_params=_cparams(("arbitrary", "arbitrary")),
        name=f"dec_gate_l{layer}",
    )(page_table, qrep, kc)


def _dec_moba_kernel(pt_ref, sel_ref, qrep_ref, knrep_ref, vnrep_ref, kc_ref, vc_ref, o_ref, kbuf, vbuf, sem,
                     *, layer):
    seq = pl.program_id(0)
    slot = lax.rem(seq, 2)
    per_blk = MOBA_BLOCK // kbuf.shape[3]
    per_head = MOBA_TOPK * per_blk

    def copies(s, sl):
        out = []
        for hd in range(MOBA_HEADS):
            rows = pl.ds(hd * HEAD_DIM, HEAD_DIM)
            for r in range(MOBA_TOPK):
                blk = sel_ref[s, r * MOBA_HEADS + hd]
                for pg in range(per_blk):
                    page = pt_ref[s, per_blk * blk + pg]
                    j = hd * per_head + r * per_blk + pg
                    out.append(pltpu.make_async_copy(kc_ref.at[layer, page, rows], kbuf.at[sl, j], sem.at[0, sl]))
                    out.append(pltpu.make_async_copy(vc_ref.at[layer, page, rows], vbuf.at[sl, j], sem.at[1, sl]))
        return out

    @pl.when(seq == 0)
    def _prime():
        for cp in copies(0, 0):
            cp.start()

    @pl.when(seq + 1 < pl.num_programs(0))
    def _prefetch():
        for cp in copies(seq + 1, 1 - slot):
            cp.start()

    for cp in copies(seq, slot):
        cp.wait()

    for hd in range(MOBA_HEADS):
        r0 = hd * HEAD_DIM
        q = qrep_ref[0, r0:r0 + HEAD_DIM, :]
        s_self = jnp.sum(q * knrep_ref[0, r0:r0 + HEAD_DIM, :], axis=0, keepdims=True)
        s_pages = [jnp.sum(kbuf[slot, hd * per_head + j] * q, axis=0, keepdims=True) for j in range(per_head)]
        m = s_self
        for s in s_pages:
            m = jnp.maximum(m, jnp.max(s, axis=1, keepdims=True))
        p_self = jnp.exp(s_self - m)
        l = p_self
        o = vnrep_ref[0, r0:r0 + HEAD_DIM, :] * p_self
        for j, s in enumerate(s_pages):
            p = jnp.exp(s - m)
            l = l + jnp.sum(p, axis=1, keepdims=True)
            o = o + jnp.sum(vbuf[slot, hd * per_head + j] * p, axis=1, keepdims=True)
        o_ref[0, r0:r0 + HEAD_DIM, :] = o / l


def _dec_moba_call(layer, page_table, sel, qrep, knrep, vnrep, kc, vc):
    nb = page_table.shape[0]
    rows, page = kc.shape[2], kc.shape[3]
    n_slab = MOBA_HEADS * MOBA_TOPK * (MOBA_BLOCK // page)
    rep_spec = pl.BlockSpec((1, rows, LANES), lambda i, pt, sl: (i, 0, 0))
    return pl.pallas_call(
        functools.partial(_dec_moba_kernel, layer=layer),
        grid_spec=pltpu.PrefetchScalarGridSpec(
            num_scalar_prefetch=2,
            grid=(nb,),
            in_specs=[rep_spec, rep_spec, rep_spec,
                      pl.BlockSpec(memory_space=pl.ANY), pl.BlockSpec(memory_space=pl.ANY)],
            out_specs=rep_spec,
            scratch_shapes=[pltpu.VMEM((2, n_slab, HEAD_DIM, page), F32),
                            pltpu.VMEM((2, n_slab, HEAD_DIM, page), F32),
                            pltpu.SemaphoreType.DMA((2, 2))],
        ),
        out_shape=jax.ShapeDtypeStruct((nb, rows, LANES), F32),
        compiler_params=_cparams(("arbitrary",)),
        name=f"dec_moba_l{layer}",
    )(page_table, sel, qrep, knrep, vnrep, kc, vc)


def _dec_diff_kernel(pt_ref, lam_ref, g_ref, qrep_ref, knrep_ref, vnew_ref, kc_ref, vc_ref, o_ref,
                     kbuf, vbuf, sem, m_s, l_s, acc_s, *, layer, n, lam_init):
    for part in _dec_diff_step(pt_ref, lam_ref, g_ref, qrep_ref, knrep_ref, vnew_ref, kc_ref, vc_ref, o_ref,
                               kbuf, vbuf, sem, m_s, l_s, acc_s, layer=layer, n=n, lam_init=lam_init,
                               pos=_grid_pos()):
        part()


def _dec_diff_step(pt_ref, lam_ref, g_ref, qrep_ref, knrep_ref, vnew_ref, kc_ref, vc_ref, o_ref,
                   kbuf, vbuf, sem, m_s, l_s, acc_s, *, layer, n, lam_init, pos):
    chunk, n_chunk = pos[1], pos[2]
    page = kbuf.shape[3]
    ready = []

    def stream():
        ready.append(_stream_chunks(pt_ref, (kc_ref, vc_ref), (kbuf, vbuf), sem, layer, n, pos))

        @pl.when(chunk == 0)
        def _init():
            m_s[...] = jnp.full(m_s.shape, -jnp.inf, F32)
            l_s[...] = jnp.zeros_like(l_s)
            acc_s[...] = jnp.zeros_like(acc_s)

    def reduce():
        slot = ready[0]
        qrep = qrep_ref[0]
        for first in range(0, n, DEC_GROUP_PAGES):
            pages = range(first, min(first + DEC_GROUP_PAGES, n))
            s = jnp.concatenate([_head_scores(kbuf[slot, i], qrep) for i in pages], axis=1)
            m_prev = m_s[...]
            m_next = jnp.maximum(m_prev, jnp.max(s, axis=1, keepdims=True))
            alpha = jnp.exp(m_prev - m_next)
            p = jnp.exp(s - jnp.concatenate([m_next] * len(pages), axis=1))
            l_s[...] = alpha * l_s[...] + jnp.sum(p, axis=1, keepdims=True)
            m_s[...] = m_next
            pb = p.astype(BF16)
            for hd in range(DIFF_HEADS):
                v = jnp.concatenate(
                    [vbuf[slot, i, pl.ds(hd, page, stride=DIFF_HEADS), :].astype(BF16) for i in pages], axis=0)
                acc_s[hd] = acc_s[hd] * alpha + jnp.dot(pb, v, preferred_element_type=F32)

    def finish():
        @pl.when(chunk == n_chunk - 1)
        def _finish():
            n_map = 2 * DIFF_HEADS
            s_self = jnp.sum((qrep_ref[0] * knrep_ref[0]).reshape(n_map, HEAD_DIM, LANES), axis=1)
            m_prev = m_s[...]
            m_fin = jnp.maximum(m_prev, s_self)
            alpha = jnp.exp(m_prev - m_fin)
            p_self = jnp.exp(s_self - m_fin)
            l_fin = alpha * l_s[...] + p_self
            lam = _lambda(lam_ref, lam_init)
            for hd in range(DIFF_HEADS):
                maps = (acc_s[hd] * alpha + p_self * vnew_ref[0, hd:hd + 1, :]) / l_fin
                o = maps[2 * hd:2 * hd + 1, :] - lam * maps[2 * hd + 1:2 * hd + 2, :]
                o_ref[0, hd:hd + 1, :] = _rmsnorm(o, g_ref[...]) * (1.0 - lam_init)

    return stream, reduce, finish


def _dec_diff_call(layer, lam_init, page_table, lamv, g_sub, qrep, knrep, vnew, kc, vc, n):
    nb, n_pages = page_table.shape
    rows, page = kc.shape[2], kc.shape[3]
    n_map = 2 * DIFF_HEADS
    full = lambda a: pl.BlockSpec(a.shape, lambda i, c, pt: (0, 0))
    rep_spec = pl.BlockSpec((1, rows, LANES), lambda i, c, pt: (i, 0, 0))
    head_spec = pl.BlockSpec((1, DIFF_HEADS, LANES), lambda i, c, pt: (i, 0, 0))
    return pl.pallas_call(
        functools.partial(_dec_diff_kernel, layer=layer, n=n, lam_init=lam_init),
        grid_spec=pltpu.PrefetchScalarGridSpec(
            num_scalar_prefetch=1,
            grid=(nb, n_pages // n),
            in_specs=[full(lamv), full(g_sub), rep_spec, rep_spec, head_spec,
                      pl.BlockSpec(memory_space=pl.ANY), pl.BlockSpec(memory_space=pl.ANY)],
            out_specs=head_spec,
            scratch_shapes=[pltpu.VMEM((2, n, rows, page), F32),
                            pltpu.VMEM((2, n, vc.shape[2], vc.shape[3]), F32),
                            pltpu.SemaphoreType.DMA((2, 2)),
                            pltpu.VMEM((n_map, LANES), F32), pltpu.VMEM((n_map, LANES), F32),
                            pltpu.VMEM((DIFF_HEADS, n_map, LANES), F32)],
        ),
        out_shape=jax.ShapeDtypeStruct((nb, DIFF_HEADS, LANES), F32),
        compiler_params=_cparams(("arbitrary", "arbitrary")),
        name=f"dec_diff_l{layer}",
    )(page_table, lamv, g_sub, qrep, knrep, vnew, kc, vc)


def _rope_tables(pos):
    inv = ROPE_THETA ** (-jnp.arange(HALF_DIM, dtype=F32) / HALF_DIM)
    ang = pos.astype(F32)[:, None] * inv[None, :]
    cos, sin = jnp.cos(ang), jnp.sin(ang)
    reps = LANES // HEAD_DIM
    cos_r = jnp.tile(jnp.concatenate([cos, cos], axis=1), (1, reps))
    sin_r = jnp.tile(jnp.concatenate([-sin, sin], axis=1), (1, reps))
    return cos_r, sin_r, cos.T, sin.T


def _lane_rep(a):
    return jnp.broadcast_to(a[:, :, None], a.shape + (LANES,))


def _pages_per_chunk(n_pages, page):
    per_blk = MOBA_BLOCK // page
    for n in (16, 8, 4, 2):
        if n_pages % n == 0 and n % per_blk == 0:
            return n
    raise ValueError("page count must be a multiple of the pages per MoBA block")


def _fused_chunk_pages(total_pages, attn_steps, n_pages, page):
    per_blk = MOBA_BLOCK // page
    n = total_pages // attn_steps
    ok = (n * attn_steps == total_pages and n > 0 and n_pages % n == 0 and n % per_blk == 0
          and n <= MAX_CHUNK_PAGES and DIFF_HEADS == GROUP_W // LANES)
    return n if ok else None


def kernel(x_prompt, x_sample, cache_k_moba, cache_v_moba, cache_k_diff, cache_v_diff, page_table, c_prompt, c_sample, w_in, w_out, g_attn, g_mlp, w_ada, b_ada, w_up, w_down, lambda_q1, lambda_k1, lambda_q2, lambda_k2, g_subln, g_final):
    b, t, d = x_prompt.shape
    nb, dec_seq, _ = x_sample.shape
    depth = w_in.shape[0]
    n_pool, page = cache_k_moba.shape[1], cache_k_moba.shape[2]
    n_pages = page_table.shape[1]
    assert d == D_MODEL and dec_seq == 1 and t % (2 * MOBA_BLOCK) == 0
    assert MOBA_BLOCK % page == 0 and page == LANES and n_pages * page // MOBA_BLOCK >= MOBA_TOPK
    n_chunk_pages = _pages_per_chunk(n_pages, page)
    fused_pages = _fused_chunk_pages(nb * n_pages, b * (GROUP_W // LANES) * (t // ATTN_TQ), n_pages, page)

    to_kt = lambda c: jnp.transpose(c, (0, 1, 3, 4, 2)).reshape(depth, n_pool, GROUP_W, page)
    kc_m, vc_m, kc_d = to_kt(cache_k_moba), to_kt(cache_v_moba), to_kt(cache_k_diff)
    vc_d = cache_v_diff.reshape(depth, n_pool, page * DIFF_HEADS, 2 * HEAD_DIM)

    mod = _ada_call(jnp.concatenate([c_prompt, c_sample], axis=0), w_ada, b_ada)
    _, _, cos_t, sin_t = _rope_tables(jnp.arange(t, dtype=jnp.int32))
    cos_s, sin_s, _, _ = _rope_tables(jnp.full((nb,), n_pages * page, jnp.int32))

    xp = x_prompt
    xs = x_sample.reshape(1, nb, d)
    layered = None
    sample_new = []
    for l in range(depth):
        lam_init = 0.8 - 0.6 * math.exp(-0.3 * l)
        final = l == depth - 1
        lamv = jnp.stack([lambda_q1[l], lambda_k1[l], lambda_q2[l], lambda_k2[l]])
        g_sub = g_subln[l].reshape(1, -1)
        g_a, g_m = g_attn[l].reshape(1, d), g_mlp[l].reshape(1, d)
        g_f = g_final.reshape(1, d)
        mods_p = [m.reshape(b, 1, d) for m in jnp.split(mod[l, :b], 6, axis=-1)]
        mods_s = [m.reshape(1, nb, d) for m in jnp.split(mod[l, b:], 6, axis=-1)]
        win = w_in[l]
        groups = [win[:, k * GROUP_W:(k + 1) * GROUP_W] for k in range(6)]
        wa = groups[5].astype(BF16)
        wbt = jnp.concatenate(groups[:5], axis=1).T.astype(BF16)

        sh_a, sc_a, gt_a, sh_m, sc_m, gt_m = mods_p
        qm, qd, vd_all, ktm_all, vtm_all, ktd_all = _proj_call(
            l, depth, xp, g_a, sc_a, sh_a, wa, wbt, cos_t, sin_t, layered)
        layered = (vd_all, ktm_all, vtm_all, ktd_all)
        ps = _sproj_call(l, xs[0], g_a, mods_s[1][0], mods_s[0][0], win, cos_s, sin_s)
        qm_s, km_s, vm_s, qd_s, kd_s, vd_s = [ps[:, k * GROUP_W:(k + 1) * GROUP_W] for k in range(6)]
        sample_new.append((km_s, vm_s, kd_s, vd_s))
        qrep_m = _lane_rep(qm_s)
        dec_diff_args = (_lane_rep(qd_s), _lane_rep(kd_s), vd_s.reshape(nb, DIFF_HEADS, 2 * HEAD_DIM), kc_d, vc_d)

        if fused_pages is not None:
            om, sel = _moba_call(l, qm, ktm_all, vtm_all, (page_table, qrep_m, kc_m, fused_pages))
            od, od_s = _diff_call(l, lam_init, lamv, g_sub, qd, ktd_all, vd_all,
                                  (page_table,) + dec_diff_args + (fused_pages,))
        else:
            om = _moba_call(l, qm, ktm_all, vtm_all)
            od = _diff_call(l, lam_init, lamv, g_sub, qd, ktd_all, vd_all)
            sel = _dec_gate_call(l, page_table, qrep_m, kc_m, n_chunk_pages)
            od_s = _dec_diff_call(l, lam_init, page_table, lamv, g_sub, *dec_diff_args, n_chunk_pages)
        om_s = _dec_moba_call(l, page_table, sel[:, :, :, 0].reshape(nb, MOBA_TOPK * MOBA_HEADS), qrep_m,
                              _lane_rep(km_s), _lane_rep(vm_s), kc_m, vc_m)[:, :, 0]

        x1, h2 = _outproj_call(f"outproj_l{l}", xp, om, od, w_out[l].astype(BF16), gt_a, g_m, sc_m, sh_m,
                               min(PROJ_TM, t), False)
        xp = _mlp_call(f"mlp_l{l}", h2, x1, w_up[l].astype(BF16), w_down[l].astype(BF16), gt_m, g_f,
                       min(MLP_TM, t), False, final)
        sh_a, sc_a, gt_a, sh_m, sc_m, gt_m = mods_s
        x1s, h2s = _outproj_call(f"outproj_s_l{l}", xs, om_s.reshape(1, nb, GROUP_W), od_s.reshape(1, nb, GROUP_W),
                                 w_out[l], gt_a, g_m, sc_m, sh_m, nb, True)
        xs = _mlp_call(f"mlp_s_l{l}", h2s, x1s, w_up[l], w_down[l], gt_m, g_f, nb, True, final)

    vd_all, ktm_all, vtm_all, ktd_all = layered
    from_kt = lambda a: jnp.transpose(a.reshape(depth, b, GROUP_W // HEAD_DIM, HEAD_DIM, t), (0, 1, 4, 2, 3))
    stack_s = lambda k, shape: jnp.stack([s[k] for s in sample_new]).reshape((depth, nb, 1) + shape)
    return (
        xp,
        xs.reshape(nb, 1, d),
        from_kt(ktm_all), from_kt(vtm_all), from_kt(ktd_all),
        vd_all.reshape(depth, b, t, DIFF_HEADS, 2 * HEAD_DIM),
        stack_s(0, (MOBA_HEADS, HEAD_DIM)), stack_s(1, (MOBA_HEADS, HEAD_DIM)),
        stack_s(2, (2 * DIFF_HEADS, HEAD_DIM)), stack_s(3, (DIFF_HEADS, 2 * HEAD_DIM)),
    )
```
